```python
import math
import jax, jax.numpy as jnp
from jax import lax
import numpy as np

D_MODEL = 1024
BATCH = 4
SEQ = 4096
DEPTH = 2

HEAD_DIM = 64
DIFF_HEADS = D_MODEL // (4 * HEAD_DIM)
FOX_HEADS = D_MODEL // (2 * HEAD_DIM)
DIFF_QK = DIFF_HEADS * 2 * HEAD_DIM
DIFF_V = DIFF_HEADS * 2 * HEAD_DIM
FOX_QK = FOX_HEADS * HEAD_DIM
FOX_V = FOX_HEADS * HEAD_DIM
ATTN_IN_SPLIT = (DIFF_QK, DIFF_QK, DIFF_V, FOX_QK, FOX_QK, FOX_V, FOX_HEADS)
ATTN_IN_WIDTH = sum(ATTN_IN_SPLIT)
ATTN_OUT_WIDTH = DIFF_V + FOX_V
Q_BLOCK = 128
FORGET_BIAS_INIT = 3.0
RNN_BLOCK_W = 128
RNN_WIDTH = (4 * D_MODEL // 3) // RNN_BLOCK_W * RNN_BLOCK_W
RNN_BLOCKS = RNN_WIDTH // RNN_BLOCK_W
CONV_WIDTH = 4
RG_C = 8.0
D_FF = 4 * D_MODEL
PLE_DIM = 256
N_ATTN_LAYERS = (DEPTH + 1) // 2
N_REC_LAYERS = DEPTH // 2
NORM_EPS = 1e-6
SUBLN_EPS = 1e-5

kernel_name = "hybrid_diff_fox_rglru_block"


def rmsnorm(x, gain, eps=NORM_EPS):
    xf = x.astype(jnp.float32)
    y = xf * lax.rsqrt(jnp.mean(xf * xf, axis=-1, keepdims=True) + eps)
    return (y * gain.astype(jnp.float32)).astype(x.dtype)


def alibi_slopes(n_heads):
    return jnp.exp2(-8.0 * jnp.arange(1, n_heads + 1, dtype=jnp.float32) / n_heads)


def sweep_query_blocks(block_fn, batch, seq):
    out = lax.map(block_fn, jnp.arange(seq // Q_BLOCK))
    return jnp.moveaxis(out, 0, 1).reshape(batch, seq, out.shape[3], out.shape[4])


def differential_attention(q, k, v, lam):
    B, S, H, _, Dh = q.shape
    qf = q.astype(jnp.float32) * (Dh ** -0.5)
    kf = k.astype(jnp.float32)
    vf = v.astype(jnp.float32)
    slopes = alibi_slopes(H)
    kpos = jnp.arange(S)

    def block(i):
        qb = lax.dynamic_slice_in_dim(qf, i * Q_BLOCK, Q_BLOCK, axis=1)
        qpos = i * Q_BLOCK + jnp.arange(Q_BLOCK)
        s = jnp.einsum('bqhmd,bkhmd->bhmqk', qb, kf)
        dist = (qpos[:, None] - kpos[None, :]).astype(jnp.float32)
        s = s - (slopes[:, None, None] * dist)[None, :, None]
        causal = kpos[None, :] <= qpos[:, None]
        s = jnp.where(causal[None, None, None], s, -jnp.inf)
        pr = jax.nn.softmax(s, axis=-1)
        w = pr[:, :, 0] - lam * pr[:, :, 1]
        return jnp.einsum('bhqk,bkhe->bqhe', w, vf)

    return sweep_query_blocks(block, B, S)


def forgetting_attention(q, k, v, cum_log_f):
    B, S, H, Dh = q.shape
    qf = q.astype(jnp.float32) * (Dh ** -0.5)
    kf = k.astype(jnp.float32)
    vf = v.astype(jnp.float32)
    c_t = jnp.transpose(cum_log_f, (0, 2, 1))
    kpos = jnp.arange(S)

    def block(i):
        qb = lax.dynamic_slice_in_dim(qf, i * Q_BLOCK, Q_BLOCK, axis=1)
        cq = lax.dynamic_slice_in_dim(c_t, i * Q_BLOCK, Q_BLOCK, axis=2)
        qpos = i * Q_BLOCK + jnp.arange(Q_BLOCK)
        s = jnp.einsum('bqhd,bkhd->bhqk', qb, kf)
        s = s + (cq[..., :, None] - c_t[:, :, None, :])
        causal = kpos[None, :] <= qpos[:, None]
        s = jnp.where(causal[None, None], s, -jnp.inf)
        pr = jax.nn.softmax(s, axis=-1)
        return jnp.einsum('bhqk,bkhd->bqhd', pr, vf)

    return sweep_query_blocks(block, B, S)


def attention_mixer(hn, w_in, b_forget, w_out, lq1, lk1, lq2, lk2, subln, layer):
    B, S, _ = hn.shape
    z = hn @ w_in
    idx = [int(v) for v in np.cumsum(ATTN_IN_SPLIT)[:-1]]
    dq, dk, dv, fq, fk, fv, fz = jnp.split(z, idx, axis=-1)
    lam_init = 0.8 - 0.6 * math.exp(-0.3 * layer)
    lam = (jnp.exp(jnp.sum(lq1.astype(jnp.float32) * lk1.astype(jnp.float32)))
           - jnp.exp(jnp.sum(lq2.astype(jnp.float32) * lk2.astype(jnp.float32)))
           + lam_init)
    d_out = differential_attention(dq.reshape(B, S, DIFF_HEADS, 2, HEAD_DIM),
                                   dk.reshape(B, S, DIFF_HEADS, 2, HEAD_DIM),
                                   dv.reshape(B, S, DIFF_HEADS, 2 * HEAD_DIM), lam)
    d_out = rmsnorm(d_out, subln, eps=SUBLN_EPS) * (1.0 - lam_init)
    log_f = jax.nn.log_sigmoid(fz.astype(jnp.float32) + b_forget.astype(jnp.float32))
    cum_log_f = jnp.cumsum(log_f, axis=1)
    f_out = forgetting_attention(fq.reshape(B, S, FOX_HEADS, HEAD_DIM),
                                 fk.reshape(B, S, FOX_HEADS, HEAD_DIM),
                                 fv.reshape(B, S, FOX_HEADS, HEAD_DIM), cum_log_f)
    o = jnp.concatenate([d_out.reshape(B, S, DIFF_V), f_out.reshape(B, S, FOX_V)], axis=-1)
    return o.astype(hn.dtype) @ w_out


def _linear_recurrence_combine(earlier, later):
    a1, b1 = earlier
    a2, b2 = later
    return a1 * a2, a2 * b1 + b2


def recurrent_mixer(hn, w_in, conv_w, conv_b, wx, bx, wa, ba, a_param, w_out):
    B, S, _ = hn.shape
    gate_branch, xr = jnp.split(hn @ w_in, 2, axis=-1)
    y = jax.nn.gelu(gate_branch)
    xc = lax.conv_general_dilated(
        xr, conv_w.reshape(CONV_WIDTH, 1, RNN_WIDTH).astype(xr.dtype),
        window_strides=(1,), padding=[(CONV_WIDTH - 1, 0)],
        dimension_numbers=('NWC', 'WIO', 'NWC'),
        feature_group_count=RNN_WIDTH) + conv_b
    xb = xc.reshape(B, S, RNN_BLOCKS, RNN_BLOCK_W)
    gate_x = jax.nn.sigmoid(jnp.einsum('bsni,nij->bsnj', xb, wx).reshape(B, S, RNN_WIDTH) + bx)
    gate_a = jax.nn.sigmoid(jnp.einsum('bsni,nij->bsnj', xb, wa).reshape(B, S, RNN_WIDTH) + ba)
    log_a = RG_C * gate_a.astype(jnp.float32) * jax.nn.log_sigmoid(a_param.astype(jnp.float32))
    a = jnp.exp(log_a)
    mult = jnp.sqrt(-jnp.expm1(2.0 * log_a))
    mult = jnp.where((jnp.arange(S) == 0)[None, :, None], 1.0, mult)
    b = mult * gate_x.astype(jnp.float32) * xc.astype(jnp.float32)
    _, h = lax.associative_scan(_linear_recurrence_combine, (a, b), axis=1)
    return (h.astype(hn.dtype) * y) @ w_out


def setup_inputs(seed: int = 0) -> dict:
    key = jax.random.key(seed)
    ks = iter(jax.random.split(key, 40))

    def nrm(shape, scale):
        return jax.random.normal(next(ks), shape, jnp.float32) * scale

    def gain(shape):
        return 1.0 + nrm(shape, 0.02)

    NA, NR = N_ATTN_LAYERS, N_REC_LAYERS
    u = jax.random.uniform(next(ks), (NR, RNN_WIDTH), jnp.float32, 0.9, 0.999)
    s = u ** (1.0 / RG_C)
    a_param = jnp.log(s) - jnp.log1p(-s)
    return {
        "x": nrm((BATCH, SEQ, D_MODEL), 1.0),
        "p": nrm((DEPTH, BATCH, SEQ, PLE_DIM), 1.0),
        "ln_mix_pre": gain((DEPTH, D_MODEL)),
        "ln_mix_post": gain((DEPTH, D_MODEL)),
        "ln_mlp_pre": gain((DEPTH, D_MODEL)),
        "ln_mlp_post": gain((DEPTH, D_MODEL)),
        "mlp_w_up": nrm((DEPTH, D_MODEL, D_FF), D_MODEL ** -0.5),
        "mlp_w_down": nrm((DEPTH, D_FF, D_MODEL), D_FF ** -0.5),
        "ple_w_proj": nrm((DEPTH, PLE_DIM, D_MODEL), PLE_DIM ** -0.5),
        "ple_norm": gain((DEPTH, D_MODEL)),
        "ple_w_gate": nrm((DEPTH, D_MODEL, D_MODEL), D_MODEL ** -0.5),
        "attn_w_in": nrm((NA, D_MODEL, ATTN_IN_WIDTH), D_MODEL ** -0.5),
        "attn_b_forget": FORGET_BIAS_INIT + nrm((NA, FOX_HEADS), 0.1),
        "attn_w_out": nrm((NA, ATTN_OUT_WIDTH, D_MODEL), ATTN_OUT_WIDTH ** -0.5),
        "diff_lambda_q1": nrm((NA, HEAD_DIM), 0.1),
        "diff_lambda_k1": nrm((NA, HEAD_DIM), 0.1),
        "diff_lambda_q2": nrm((NA, HEAD_DIM), 0.1),
        "diff_lambda_k2": nrm((NA, HEAD_DIM), 0.1),
        "diff_subln": gain((NA, 2 * HEAD_DIM)),
        "rec_w_in": nrm((NR, D_MODEL, 2 * RNN_WIDTH), D_MODEL ** -0.5),
        "rec_conv_w": nrm((NR, CONV_WIDTH, RNN_WIDTH), CONV_WIDTH ** -0.5),
        "rec_conv_b": nrm((NR, RNN_WIDTH), 0.01),
        "rec_wx": nrm((NR, RNN_BLOCKS, RNN_BLOCK_W, RNN_BLOCK_W), RNN_BLOCK_W ** -0.5),
        "rec_bx": nrm((NR, RNN_WIDTH), 0.01),
        "rec_wa": nrm((NR, RNN_BLOCKS, RNN_BLOCK_W, RNN_BLOCK_W), RNN_BLOCK_W ** -0.5),
        "rec_ba": nrm((NR, RNN_WIDTH), 0.01),
        "rec_a_param": a_param,
        "rec_w_out": nrm((NR, RNN_WIDTH, D_MODEL), RNN_WIDTH ** -0.5),
    }


def reference(x, p, ln_mix_pre, ln_mix_post, ln_mlp_pre, ln_mlp_post, mlp_w_up, mlp_w_down,
              ple_w_proj, ple_norm, ple_w_gate, attn_w_in, attn_b_forget, attn_w_out,
              diff_lambda_q1, diff_lambda_k1, diff_lambda_q2, diff_lambda_k2, diff_subln,
              rec_w_in, rec_conv_w, rec_conv_b, rec_wx, rec_bx, rec_wa, rec_ba,
              rec_a_param, rec_w_out):
    h = x
    for layer in range(DEPTH):
        j = layer // 2
        hn = rmsnorm(h, ln_mix_pre[layer])
        if layer % 2 == 0:
            m = attention_mixer(hn, attn_w_in[j], attn_b_forget[j], attn_w_out[j],
                                diff_lambda_q1[j], diff_lambda_k1[j],
                                diff_lambda_q2[j], diff_lambda_k2[j], diff_subln[j], layer)
        else:
            m = recurrent_mixer(hn, rec_w_in[j], rec_conv_w[j], rec_conv_b[j], rec_wx[j],
                                rec_bx[j], rec_wa[j], rec_ba[j], rec_a_param[j], rec_w_out[j])
        h = h + rmsnorm(m, ln_mix_post[layer])
        u = rmsnorm(h, ln_mlp_pre[layer])
        f = jnp.square(jax.nn.relu(u @ mlp_w_up[layer])) @ mlp_w_down[layer]
        h = h + rmsnorm(f, ln_mlp_post[layer])
        e = rmsnorm(p[layer] @ ple_w_proj[layer], ple_norm[layer])
        h = h + e * jax.nn.sigmoid(h @ ple_w_gate[layer])
    return h
```

```python
import functools
import math

import jax
import jax.numpy as jnp
from jax import lax
from jax.experimental import pallas as pl
from jax.experimental.pallas import tpu as pltpu

F32 = jnp.float32
BF16 = jnp.bfloat16

LANES = 128
SUBLANES = 8
VMEM_LIMIT_BYTES = 56 * 1024 * 1024

HEAD_DIM = 64
UNIT = 2 * HEAD_DIM
NORM_EPS = 1e-6
SUBLN_EPS = 1e-5
RG_C = 8.0
CONV_WIDTH = 4
RNN_BLOCK_W = 128
MASKED = -1e30


def _rms(x, gain, eps=NORM_EPS):
    return x * lax.rsqrt(jnp.mean(x * x, axis=-1, keepdims=True) + eps) * gain


def _dot(a, b):
    return jnp.dot(a, b, preferred_element_type=F32)


def _dot_nt(a, b):
    return lax.dot_general(a, b, (((1,), (1,)), ((), ())), preferred_element_type=F32)


def _log_sigmoid(x):
    return jnp.minimum(x, 0.0) - jnp.log1p(jnp.exp(-jnp.abs(x)))


def _resident(shape):
    zeros = (0,) * len(shape)
    return pl.BlockSpec(shape, lambda *_: zeros, pipeline_mode=pl.Buffered(1))


def _params(semantics):
    return pltpu.CompilerParams(dimension_semantics=semantics, vmem_limit_bytes=VMEM_LIMIT_BYTES)


def _attn_in_kernel(h_ref, g_ref, w_ref, wfz_ref, bf_ref, z_ref, lf_ref, *, n_chunk):
    hn = _rms(h_ref[...], g_ref[...]).astype(BF16)
    width = w_ref.shape[1] // n_chunk
    for c in range(n_chunk):
        cols = slice(c * width, (c + 1) * width)
        z_ref[:, cols] = _dot(hn, w_ref[:, cols]).astype(BF16)
    fz = _dot_nt(wfz_ref[...], hn)
    lf_ref[...] = _log_sigmoid(fz[:SUBLANES] + bf_ref[...])


def _attn_in(h, gain, w_main, w_fz_t, b_forget, *, tm):
    T, D = h.shape
    N = w_main.shape[1]
    return pl.pallas_call(
        functools.partial(_attn_in_kernel, n_chunk=N // 1024),
        grid=(T // tm,),
        in_specs=[
            pl.BlockSpec((tm, D), lambda i: (i, 0)),
            _resident((1, D)),
            _resident((D, N)),
            _resident(w_fz_t.shape),
            _resident(b_forget.shape),
        ],
        out_specs=[
            pl.BlockSpec((tm, N), lambda i: (i, 0)),
            pl.BlockSpec((SUBLANES, tm), lambda i: (0, i)),
        ],
        out_shape=[
            jax.ShapeDtypeStruct((T, N), BF16),
            jax.ShapeDtypeStruct((SUBLANES, T), F32),
        ],
        compiler_params=_params(("parallel",)),
        name="attn_in",
    )(h, gain, w_main, w_fz_t, b_forget)


def _fox_bias_kernel(lf_ref, cb_ref, *, n_diff_heads):
    x = lf_ref[...]
    S = x.shape[1]
    pos = lax.broadcasted_iota(jnp.int32, x.shape, 1)
    k = 1
    while k < S:
        x = x + jnp.where(pos >= k, pltpu.roll(x, k, axis=1), 0.0)
        k *= 2
    row = lax.broadcasted_iota(jnp.int32, x.shape, 0)
    head = (row // 2 + 1).astype(F32)
    slope = jnp.exp2(-8.0 * head / n_diff_heads)
    cb_ref[0:SUBLANES, :] = slope * pos.astype(F32)
    cb_ref[SUBLANES:, :] = -x


def _fox_bias(lf_t, *, B, S, n_diff_heads):
    return pl.pallas_call(
        functools.partial(_fox_bias_kernel, n_diff_heads=n_diff_heads),
        grid=(B,),
        in_specs=[pl.BlockSpec((SUBLANES, S), lambda b: (0, b))],
        out_specs=pl.BlockSpec((None, 2 * SUBLANES, S), lambda b: (b, 0, 0)),
        out_shape=jax.ShapeDtypeStruct((B, 2 * SUBLANES, S), F32),
        compiler_params=_params(("parallel",)),
        name="fox_bias",
    )(lf_t)


def _attention_kernel(q_ref, k_ref, v_ref, cb_ref, lp_ref, sg_ref, o_ref, m_ref, l_ref, acc_ref,
                      *, tq, n_diff_units, lam_init):
    u = pl.program_id(1)
    qi = pl.program_id(2)
    q = q_ref[...]
    lane = lax.broadcasted_iota(jnp.int32, q.shape, 1)
    zero = jnp.zeros_like(q)
    q_half = (jnp.where(lane < HEAD_DIM, q, zero), jnp.where(lane >= HEAD_DIM, q, zero))

    m_ref[...] = jnp.full(m_ref.shape, MASKED, F32)
    l_ref[...] = jnp.zeros(l_ref.shape, F32)
    acc_ref[...] = jnp.zeros(acc_ref.shape, F32)

    def block(ki, diagonal):
        start = pl.multiple_of(ki * tq, tq)
        k = k_ref[pl.ds(start, tq), :]
        v = v_ref[pl.ds(start, tq), :]
        for s in range(2):
            sc = _dot_nt(q_half[s], k) + cb_ref[s:s + 1, pl.ds(start, tq)]
            if diagonal:
                r = lax.broadcasted_iota(jnp.int32, sc.shape, 0)
                c = lax.broadcasted_iota(jnp.int32, sc.shape, 1)
                sc = jnp.where(c <= r, sc, MASKED)
            m_old = m_ref[s]
            m_new = jnp.maximum(m_old, jnp.max(sc, axis=-1, keepdims=True))
            alpha = jnp.exp(m_old - m_new)
            p = jnp.exp(sc - m_new)
            l_ref[s] = alpha * l_ref[s] + jnp.sum(p, axis=-1, keepdims=True)
            acc_ref[s] = alpha * acc_ref[s] + _dot(p.astype(BF16), v)
            m_ref[s] = m_new

    def body(ki, carry):
        block(ki, diagonal=False)
        return carry

    lax.fori_loop(0, qi, body, 0)
    block(qi, diagonal=True)

    out_a = acc_ref[0] / l_ref[0]
    out_b = acc_ref[1] / l_ref[1]

    @pl.when(u < n_diff_units)
    def _():
        lp = lp_ref[...]
        lam = (jnp.exp(jnp.sum(lp[0:1] * lp[1:2], axis=-1, keepdims=True))
               - jnp.exp(jnp.sum(lp[2:3] * lp[3:4], axis=-1, keepdims=True)) + lam_init)
        d = out_a - lam * out_b
        o_ref[...] = (_rms(d, sg_ref[...], SUBLN_EPS) * (1.0 - lam_init)).astype(o_ref.dtype)

    @pl.when(u >= n_diff_units)
    def _():
        lane_o = lax.broadcasted_iota(jnp.int32, out_a.shape, 1)
        o_ref[...] = jnp.where(lane_o < HEAD_DIM, out_a, out_b).astype(o_ref.dtype)


def _attention(z, cb, lam_params, subln, *, B, S, tq, n_diff_units, n_units, lam_init):
    T = B * S
    nq = S // tq
    n_fox_units = n_units - n_diff_units
    def qcol(u):
        return jnp.where(u < n_diff_units, u, u + 2 * n_diff_units)

    def kcol(u):
        return jnp.where(u < n_diff_units, u + n_diff_units, u + 2 * n_diff_units + n_fox_units)

    def vcol(u):
        return jnp.where(u < n_diff_units, u + 2 * n_diff_units, u + 2 * n_diff_units + 2 * n_fox_units)

    cb4 = cb.reshape(B, n_units, 2, S)
    return pl.pallas_call(
        functools.partial(_attention_kernel, tq=tq, n_diff_units=n_diff_units, lam_init=lam_init),
        grid=(B, n_units, nq),
        in_specs=[
            pl.BlockSpec((tq, UNIT), lambda b, u, i: (b * nq + i, qcol(u))),
            pl.BlockSpec((S, UNIT), lambda b, u, i: (b, kcol(u))),
            pl.BlockSpec((S, UNIT), lambda b, u, i: (b, vcol(u))),
            pl.BlockSpec((None, None, 2, S), lambda b, u, i: (b, u, 0, 0)),
            _resident(lam_params.shape),
            _resident(subln.shape),
        ],
        out_specs=pl.BlockSpec((tq, UNIT), lambda b, u, i: (b * nq + i, u)),
        out_shape=jax.ShapeDtypeStruct((T, n_units * UNIT), BF16),
        scratch_shapes=[
            pltpu.VMEM((2, tq, 1), F32),
            pltpu.VMEM((2, tq, 1), F32),
            pltpu.VMEM((2, tq, UNIT), F32),
        ],
        compiler_params=_params(("parallel", "parallel", "arbitrary")),
        name="attention",
    )(z, z, z, cb4, lam_params, subln)


def _gelu_tanh(x):
    return 0.5 * x * (1.0 + jnp.tanh(math.sqrt(2.0 / math.pi) * (x + 0.044715 * (x * x * x))))


def _rec_kernel(h_ref, g_ref, win_ref, vec_ref, wxa_ref, o_ref, xbuf, a_buf, b_buf, hstate, *, ts, width):
    si = pl.program_id(1)
    n_blocks = width // RNN_BLOCK_W

    @pl.when(si == 0)
    def _():
        xbuf[0:SUBLANES, :] = jnp.zeros((SUBLANES, width), F32)
        hstate[...] = jnp.zeros(hstate.shape, F32)

    hn = _rms(h_ref[...], g_ref[...]).astype(BF16)
    y = _gelu_tanh(_dot(hn, win_ref[:, :width]))
    xbuf[SUBLANES:, :] = _dot(hn, win_ref[:, width:])

    xc = vec_ref[CONV_WIDTH:CONV_WIDTH + 1, :]
    for w in range(CONV_WIDTH):
        off = SUBLANES - (CONV_WIDTH - 1) + w
        xc = xc + vec_ref[w:w + 1, :] * xbuf[off:off + ts, :]
    xbuf[0:SUBLANES, :] = xbuf[ts:ts + SUBLANES, :]

    log_sig_a = _log_sigmoid(vec_ref[7:8, :])
    row = lax.broadcasted_iota(jnp.int32, (ts, RNN_BLOCK_W), 0)
    first = jnp.logical_and(row == 0, si == 0)
    for n in range(n_blocks):
        cols = slice(n * RNN_BLOCK_W, (n + 1) * RNN_BLOCK_W)
        xcn = xc[:, cols]
        g = _dot(xcn.astype(BF16), wxa_ref[n])
        gate_x = jax.nn.sigmoid(g[:, :RNN_BLOCK_W] + vec_ref[5:6, cols])
        gate_a = jax.nn.sigmoid(g[:, RNN_BLOCK_W:] + vec_ref[6:7, cols])
        log_a = RG_C * gate_a * log_sig_a[:, cols]
        a_n = jnp.exp(log_a)
        mult = jnp.sqrt(1.0 - a_n * a_n)
        mult = jnp.where(first, 1.0, mult)
        a_buf[:, cols] = a_n
        b_buf[:, cols] = mult * gate_x * xcn

    a = a_buf[...]
    b = b_buf[...]
    sub = lax.broadcasted_iota(jnp.int32, a.shape, 0) % SUBLANES
    k = 1
    while k < SUBLANES:
        keep = sub >= k
        b = jnp.where(keep, a * pltpu.roll(b, k, axis=0) + b, b)
        a = jnp.where(keep, a * pltpu.roll(a, k, axis=0), a)
        k *= 2
    a_buf[...] = a
    b_buf[...] = b

    carry = hstate[...]
    for gidx in range(ts // SUBLANES):
        rows = slice(gidx * SUBLANES, (gidx + 1) * SUBLANES)
        hg = a_buf[rows, :] * carry + b_buf[rows, :]
        o_ref[rows, :] = (hg * y[rows, :]).astype(o_ref.dtype)
        carry = hg[SUBLANES - 1:SUBLANES, :]
    hstate[...] = carry


def _rec_mixer(h, gain, w_in, vecs, wxa, *, B, S, ts):
    T, D = h.shape
    width = w_in.shape[1] // 2
    ns = S // ts
    return pl.pallas_call(
        functools.partial(_rec_kernel, ts=ts, width=width),
        grid=(B, ns),
        in_specs=[
            pl.BlockSpec((ts, D), lambda b, s: (b * ns + s, 0)),
            _resident((1, D)),
            _resident(w_in.shape),
            _resident(vecs.shape),
            _resident(wxa.shape),
        ],
        out_specs=pl.BlockSpec((ts, width), lambda b, s: (b * ns + s, 0)),
        out_shape=jax.ShapeDtypeStruct((T, width), BF16),
        scratch_shapes=[
            pltpu.VMEM((ts + SUBLANES, width), F32),
            pltpu.VMEM((ts, width), F32),
            pltpu.VMEM((ts, width), F32),
            pltpu.VMEM((1, width), F32),
        ],
        compiler_params=_params(("arbitrary", "arbitrary")),
        name="rec_mixer",
    )(h, gain, w_in, vecs, wxa)


def _post_kernel(a_ref, h_ref, p_ref, g_ref, wout_ref, wup_ref, wdn_ref, wproj_ref, wgate_ref, o_ref,
                 *, ff_chunk):
    m = _dot(a_ref[...], wout_ref[...])
    h1 = h_ref[...] + _rms(m, g_ref[0:1, :])
    u = _rms(h1, g_ref[1:2, :]).astype(BF16)
    f = jnp.zeros_like(h1)
    for c in range(wup_ref.shape[1] // ff_chunk):
        cols = slice(c * ff_chunk, (c + 1) * ff_chunk)
        t = jnp.square(jnp.maximum(_dot(u, wup_ref[:, cols]), 0.0)).astype(BF16)
        f = f + _dot(t, wdn_ref[cols, :])
    h2 = h1 + _rms(f, g_ref[2:3, :])
    e = _rms(_dot(p_ref[...].astype(BF16), wproj_ref[...]), g_ref[3:4, :])
    gate = jax.nn.sigmoid(_dot(h2.astype(BF16), wgate_ref[...]))
    o_ref[...] = h2 + e * gate


def _post(a, h, p, gains, w_out, w_up, w_down, w_proj, w_gate, *, tm):
    T, D = h.shape
    row = lambda i: (i, 0)
    return pl.pallas_call(
        functools.partial(_post_kernel, ff_chunk=1024),
        grid=(T // tm,),
        in_specs=[
            pl.BlockSpec((tm, a.shape[1]), row),
            pl.BlockSpec((tm, D), row),
            pl.BlockSpec((tm, p.shape[1]), row),
            _resident(gains.shape),
            _resident(w_out.shape),
            _resident(w_up.shape),
            _resident(w_down.shape),
            _resident(w_proj.shape),
            _resident(w_gate.shape),
        ],
        out_specs=pl.BlockSpec((tm, D), row),
        out_shape=jax.ShapeDtypeStruct((T, D), F32),
        compiler_params=_params(("parallel",)),
        name="post",
    )(a, h, p, gains, w_out, w_up, w_down, w_proj, w_gate)


def _pad_rows(x, rows):
    return jnp.concatenate([x, jnp.zeros((rows - x.shape[0],) + x.shape[1:], x.dtype)], axis=0)


def _post_layer(a, h, p, layer, ln_mix_post, ln_mlp_pre, ln_mlp_post, ple_norm, w_out,
                mlp_w_up, mlp_w_down, ple_w_proj, ple_w_gate, *, tm):
    gains = _pad_rows(jnp.stack([ln_mix_post[layer], ln_mlp_pre[layer], ln_mlp_post[layer],
                                 ple_norm[layer]]), SUBLANES)
    return _post(a, h, p, gains, w_out.astype(BF16), mlp_w_up[layer].astype(BF16),
                 mlp_w_down[layer].astype(BF16), ple_w_proj[layer].astype(BF16),
                 ple_w_gate[layer].astype(BF16), tm=tm)


def kernel(x, p, ln_mix_pre, ln_mix_post, ln_mlp_pre, ln_mlp_post, mlp_w_up, mlp_w_down, ple_w_proj, ple_norm, ple_w_gate, attn_w_in, attn_b_forget, attn_w_out, diff_lambda_q1, diff_lambda_k1, diff_lambda_q2, diff_lambda_k2, diff_subln, rec_w_in, rec_conv_w, rec_conv_b, rec_wx, rec_bx, rec_wa, rec_ba, rec_a_param, rec_w_out):
    B, S, D = x.shape
    T = B * S
    depth = p.shape[0]
    tm = 512
    n_diff_heads = D // (4 * HEAD_DIM)
    n_fox_heads = D // (2 * HEAD_DIM)
    n_units = n_diff_heads + n_fox_heads // 2
    qkv_width = 3 * (n_diff_heads + n_fox_heads // 2) * UNIT
    assert S % tm == 0 and attn_w_in.shape[2] == qkv_width + n_fox_heads == qkv_width + SUBLANES

    h = x.reshape(T, D)
    for layer in range(depth):
        j = layer // 2
        gain = ln_mix_pre[layer].reshape(1, D)
        if layer % 2 == 0:
            w_in = attn_w_in[j]
            col = jnp.arange(qkv_width)
            is_q = (col < n_diff_heads * UNIT) | ((col >= 3 * n_diff_heads * UNIT)
                                                  & (col < 3 * n_diff_heads * UNIT + n_fox_heads * HEAD_DIM))
            w_main = (w_in[:, :qkv_width] * jnp.where(is_q, HEAD_DIM ** -0.5, 1.0)).astype(BF16)
            w_fz_t = _pad_rows(w_in[:, qkv_width:].T, 2 * SUBLANES).astype(BF16)
            z, lf_t = _attn_in(h, gain, w_main, w_fz_t, attn_b_forget[j].reshape(n_fox_heads, 1), tm=tm)
            cb = _fox_bias(lf_t, B=B, S=S, n_diff_heads=n_diff_heads)
            lam_params = _pad_rows(jnp.pad(
                jnp.stack([diff_lambda_q1[j], diff_lambda_k1[j], diff_lambda_q2[j], diff_lambda_k2[j]]),
                ((0, 0), (0, LANES - HEAD_DIM))), SUBLANES)
            lam_init = 0.8 - 0.6 * math.exp(-0.3 * layer)
            a = _attention(z, cb, lam_params, diff_subln[j].reshape(1, UNIT), B=B, S=S, tq=tm,
                           n_diff_units=n_diff_heads, n_units=n_units, lam_init=lam_init)
            w_out = attn_w_out[j]
        else:
            width = rec_w_in.shape[2] // 2
            vecs = jnp.concatenate([rec_conv_w[j], rec_conv_b[j][None], rec_bx[j][None],
                                    rec_ba[j][None], rec_a_param[j][None]], axis=0)
            wxa = jnp.concatenate([rec_wx[j], rec_wa[j]], axis=-1).astype(BF16)
            a = _rec_mixer(h, gain, rec_w_in[j].astype(BF16), vecs, wxa, B=B, S=S, ts=tm)
            w_out = rec_w_out[j]
        h = _post_layer(a, h, p[layer].reshape(T, -1), layer, ln_mix_post, ln_mlp_pre, ln_mlp_post,
                        ple_norm, w_out, mlp_w_up, mlp_w_down, ple_w_proj, ple_w_gate, tm=tm)
    return h.reshape(B, S, D)
```

```python
import functools
import math

import jax
import jax.numpy as jnp
from jax import lax
from jax.experimental import pallas as pl
from jax.experimental.pallas import tpu as pltpu

F32 = jnp.float32
BF16 = jnp.bfloat16

LANES = 128
SUBLANES = 8
VMEM_LIMIT_BYTES = 56 * 1024 * 1024

HEAD_DIM = 64
UNIT = 2 * HEAD_DIM
NORM_EPS = 1e-6
SUBLN_EPS = 1e-5
RG_C = 8.0
CONV_WIDTH = 4
RNN_BLOCK_W = 128
MASKED = -1e30
LOG2_E = math.log2(math.e)


def _rms(x, gain, eps=NORM_EPS):
    return x * lax.rsqrt(jnp.mean(x * x, axis=-1, keepdims=True) + eps) * gain


def _dot(a, b):
    return jnp.dot(a, b, preferred_element_type=F32)


def _dot_nt(a, b):
    return lax.dot_general(a, b, (((1,), (1,)), ((), ())), preferred_element_type=F32)


def _log_sigmoid(x):
    return jnp.minimum(x, 0.0) - jnp.log1p(jnp.exp(-jnp.abs(x)))


def _resident(shape):
    zeros = (0,) * len(shape)
    return pl.BlockSpec(shape, lambda *_: zeros, pipeline_mode=pl.Buffered(1))


def _params(semantics):
    return pltpu.CompilerParams(dimension_semantics=semantics, vmem_limit_bytes=VMEM_LIMIT_BYTES)


def _attn_in_kernel(h_ref, g_ref, w_ref, wfz_ref, bf_ref, z_ref, lf_ref, *, n_chunk):
    hn = _rms(h_ref[...], g_ref[...]).astype(BF16)
    width = w_ref.shape[1] // n_chunk
    for c in range(n_chunk):
        cols = slice(c * width, (c + 1) * width)
        z_ref[:, cols] = _dot(hn, w_ref[:, cols]).astype(BF16)
    fz = _dot_nt(wfz_ref[...], hn)
    lf_ref[...] = _log_sigmoid(fz[:SUBLANES] + bf_ref[...])


def _attn_in(h, gain, w_main, w_fz_t, b_forget, *, tm):
    T, D = h.shape
    N = w_main.shape[1]
    return pl.pallas_call(
        functools.partial(_attn_in_kernel, n_chunk=N // 1024),
        grid=(T // tm,),
        in_specs=[
            pl.BlockSpec((tm, D), lambda i: (i, 0)),
            _resident((1, D)),
            _resident((D, N)),
            _resident(w_fz_t.shape),
            _resident(b_forget.shape),
        ],
        out_specs=[
            pl.BlockSpec((tm, N), lambda i: (i, 0)),
            pl.BlockSpec((SUBLANES, tm), lambda i: (0, i)),
        ],
        out_shape=[
            jax.ShapeDtypeStruct((T, N), BF16),
            jax.ShapeDtypeStruct((SUBLANES, T), F32),
        ],
        compiler_params=_params(("parallel",)),
        name="attn_in",
    )(h, gain, w_main, w_fz_t, b_forget)


def _fox_bias_kernel(lf_ref, cb_ref, *, n_diff_heads):
    x = lf_ref[...]
    S = x.shape[1]
    pos = lax.broadcasted_iota(jnp.int32, x.shape, 1)
    k = 1
    while k < S:
        x = x + jnp.where(pos >= k, pltpu.roll(x, k, axis=1), 0.0)
        k *= 2
    row = lax.broadcasted_iota(jnp.int32, x.shape, 0)
    head = (row // 2 + 1).astype(F32)
    slope = jnp.exp2(-8.0 * head / n_diff_heads)
    cb_ref[0:SUBLANES, :] = (LOG2_E * slope) * pos.astype(F32)
    cb_ref[SUBLANES:, :] = -LOG2_E * x


def _fox_bias(lf_t, *, B, S, n_diff_heads):
    return pl.pallas_call(
        functools.partial(_fox_bias_kernel, n_diff_heads=n_diff_heads),
        grid=(B,),
        in_specs=[pl.BlockSpec((SUBLANES, S), lambda b: (0, b))],
        out_specs=pl.BlockSpec((None, 2 * SUBLANES, S), lambda b: (b, 0, 0)),
        out_shape=jax.ShapeDtypeStruct((B, 2 * SUBLANES, S), F32),
        compiler_params=_params(("parallel",)),
        name="fox_bias",
    )(lf_t)


def _attention_kernel(q_ref, k_ref, v_ref, cb_ref, lp_ref, sg_ref, o_ref, m_ref, acc_ref,
                      *, tq, n_diff_units, lam_init):
    u = pl.program_id(1)
    qi = pl.program_id(2)
    q = q_ref[...]
    lane = lax.broadcasted_iota(jnp.int32, q.shape, 1)
    zero = jnp.zeros_like(q)
    q_half = (jnp.where(lane < HEAD_DIM, q, zero), jnp.where(lane >= HEAD_DIM, q, zero))

    m_ref[...] = jnp.full(m_ref.shape, MASKED, F32)
    acc_ref[...] = jnp.zeros(acc_ref.shape, F32)

    def block(ki, diagonal):
        start = pl.multiple_of(ki * tq, tq)
        k = k_ref[pl.ds(start, tq), :]
        v = v_ref[pl.ds(start, tq), :]
        v_ones = jnp.concatenate([v, jnp.ones_like(v)], axis=1)
        for s in range(2):
            sc = _dot_nt(q_half[s], k) + cb_ref[s:s + 1, pl.ds(start, tq)]
            if diagonal:
                r = lax.broadcasted_iota(jnp.int32, sc.shape, 0)
                c = lax.broadcasted_iota(jnp.int32, sc.shape, 1)
                sc = jnp.where(c <= r, sc, MASKED)
            m_old = m_ref[s]
            m_new = jnp.maximum(m_old, jnp.max(sc, axis=-1, keepdims=True))
            alpha = jnp.exp2(m_old - m_new)
            p = jnp.exp2(sc - pltpu.repeat(m_new, tq // LANES, axis=1))
            acc_ref[s] = pltpu.repeat(alpha, 2, axis=1) * acc_ref[s] + _dot(p.astype(BF16), v_ones)
            m_ref[s] = m_new

    def body(ki, carry):
        block(ki, diagonal=False)
        return carry

    lax.fori_loop(0, qi, body, 0)
    block(qi, diagonal=True)

    out_a = acc_ref[0, :, :UNIT] / acc_ref[0, :, UNIT:]
    out_b = acc_ref[1, :, :UNIT] / acc_ref[1, :, UNIT:]

    @pl.when(u < n_diff_units)
    def _():
        lp = lp_ref[...]
        lam = (jnp.exp(jnp.sum(lp[0:1] * lp[1:2], axis=-1, keepdims=True))
               - jnp.exp(jnp.sum(lp[2:3] * lp[3:4], axis=-1, keepdims=True)) + lam_init)
        d = out_a - lam * out_b
        o_ref[...] = (_rms(d, sg_ref[...], SUBLN_EPS) * (1.0 - lam_init)).astype(o_ref.dtype)

    @pl.when(u >= n_diff_units)
    def _():
        lane_o = lax.broadcasted_iota(jnp.int32, out_a.shape, 1)
        o_ref[...] = jnp.where(lane_o < HEAD_DIM, out_a, out_b).astype(o_ref.dtype)


def _attention(z, cb, lam_params, subln, *, B, S, tq, n_diff_units, n_units, lam_init):
    T = B * S
    nq = S // tq
    n_fox_units = n_units - n_diff_units
    def qcol(u):
        return jnp.where(u < n_diff_units, u, u + 2 * n_diff_units)

    def kcol(u):
        return jnp.where(u < n_diff_units, u + n_diff_units, u + 2 * n_diff_units + n_fox_units)

    def vcol(u):
        return jnp.where(u < n_diff_units, u + 2 * n_diff_units, u + 2 * n_diff_units + 2 * n_fox_units)

    cb4 = cb.reshape(B, n_units, 2, S)
    return pl.pallas_call(
        functools.partial(_attention_kernel, tq=tq, n_diff_units=n_diff_units, lam_init=lam_init),
        grid=(B, n_units, nq),
        in_specs=[
            pl.BlockSpec((tq, UNIT), lambda b, u, i: (b * nq + i, qcol(u))),
            pl.BlockSpec((S, UNIT), lambda b, u, i: (b, kcol(u))),
            pl.BlockSpec((S, UNIT), lambda b, u, i: (b, vcol(u))),
            pl.BlockSpec((None, None, 2, S), lambda b, u, i: (b, u, 0, 0)),
            _resident(lam_params.shape),
            _resident(subln.shape),
        ],
        out_specs=pl.BlockSpec((tq, UNIT), lambda b, u, i: (b * nq + i, u)),
        out_shape=jax.ShapeDtypeStruct((T, n_units * UNIT), BF16),
        scratch_shapes=[
            pltpu.VMEM((2, tq, LANES), F32),
            pltpu.VMEM((2, tq, 2 * UNIT), F32),
        ],
        compiler_params=_params(("parallel", "parallel", "arbitrary")),
        name="attention",
    )(z, z, z, cb4, lam_params, subln)


def _gelu_tanh(x):
    return 0.5 * x * (1.0 + jnp.tanh(math.sqrt(2.0 / math.pi) * (x + 0.044715 * (x * x * x))))


def _rec_kernel(h_ref, g_ref, win_ref, vec_ref, wxa_ref, o_ref, xbuf, a_buf, b_buf, hstate, *, ts, width):
    si = pl.program_id(1)
    n_blocks = width // RNN_BLOCK_W

    @pl.when(si == 0)
    def _():
        xbuf[0:SUBLANES, :] = jnp.zeros((SUBLANES, width), F32)
        hstate[...] = jnp.zeros(hstate.shape, F32)

    hn = _rms(h_ref[...], g_ref[...]).astype(BF16)
    y = _gelu_tanh(_dot(hn, win_ref[:, :width]))
    xbuf[SUBLANES:, :] = _dot(hn, win_ref[:, width:])

    xc = vec_ref[CONV_WIDTH:CONV_WIDTH + 1, :]
    for w in range(CONV_WIDTH):
        off = SUBLANES - (CONV_WIDTH - 1) + w
        xc = xc + vec_ref[w:w + 1, :] * xbuf[off:off + ts, :]
    xbuf[0:SUBLANES, :] = xbuf[ts:ts + SUBLANES, :]

    log_sig_a = _log_sigmoid(vec_ref[7:8, :])
    row = lax.broadcasted_iota(jnp.int32, (ts, RNN_BLOCK_W), 0)
    first = jnp.logical_and(row == 0, si == 0)
    for n in range(n_blocks):
        cols = slice(n * RNN_BLOCK_W, (n + 1) * RNN_BLOCK_W)
        xcn = xc[:, cols]
        g = _dot(xcn.astype(BF16), wxa_ref[n])
        gate_x = jax.nn.sigmoid(g[:, :RNN_BLOCK_W] + vec_ref[5:6, cols])
        gate_a = jax.nn.sigmoid(g[:, RNN_BLOCK_W:] + vec_ref[6:7, cols])
        log_a = RG_C * gate_a * log_sig_a[:, cols]
        a_n = jnp.exp(log_a)
        mult = jnp.sqrt(1.0 - a_n * a_n)
        mult = jnp.where(first, 1.0, mult)
        a_buf[:, cols] = a_n
        b_buf[:, cols] = mult * gate_x * xcn

    a = a_buf[...]
    b = b_buf[...]
    sub = lax.broadcasted_iota(jnp.int32, a.shape, 0) % SUBLANES
    k = 1
    while k < SUBLANES:
        keep = sub >= k
        b = jnp.where(keep, a * pltpu.roll(b, k, axis=0) + b, b)
        a = jnp.where(keep, a * pltpu.roll(a, k, axis=0), a)
        k *= 2
    a_buf[...] = a
    b_buf[...] = b

    carry = hstate[...]
    for gidx in range(ts // SUBLANES):
        rows = slice(gidx * SUBLANES, (gidx + 1) * SUBLANES)
        hg = a_buf[rows, :] * carry + b_buf[rows, :]
        o_ref[rows, :] = (hg * y[rows, :]).astype(o_ref.dtype)
        carry = hg[SUBLANES - 1:SUBLANES, :]
    hstate[...] = carry


def _rec_mixer(h, gain, w_in, vecs, wxa, *, B, S, ts):
    T, D = h.shape
    width = w_in.shape[1] // 2
    ns = S // ts
    return pl.pallas_call(
        functools.partial(_rec_kernel, ts=ts, width=width),
        grid=(B, ns),
        in_specs=[
            pl.BlockSpec((ts, D), lambda b, s: (b * ns + s, 0)),
            _resident((1, D)),
            _resident(w_in.shape),
            _resident(vecs.shape),
            _resident(wxa.shape),
        ],
        out_specs=pl.BlockSpec((ts, width), lambda b, s: (b * ns + s, 0)),
        out_shape=jax.ShapeDtypeStruct((T, width), BF16),
        scratch_shapes=[
            pltpu.VMEM((ts + SUBLANES, width), F32),
            pltpu.VMEM((ts, width), F32),
            pltpu.VMEM((ts, width), F32),
            pltpu.VMEM((1, width), F32),
        ],
        compiler_params=_params(("arbitrary", "arbitrary")),
        name="rec_mixer",
    )(h, gain, w_in, vecs, wxa)


def _post_kernel(a_ref, h_ref, p_ref, g_ref, wout_ref, wup_ref, wdn_ref, wproj_ref, wgate_ref, o_ref,
                 *, ff_chunk):
    m = _dot(a_ref[...], wout_ref[...])
    h1 = h_ref[...] + _rms(m, g_ref[0:1, :])
    u = _rms(h1, g_ref[1:2, :]).astype(BF16)
    f = jnp.zeros_like(h1)
    for c in range(wup_ref.shape[1] // ff_chunk):
        cols = slice(c * ff_chunk, (c + 1) * ff_chunk)
        t = jnp.square(jnp.maximum(_dot(u, wup_ref[:, cols]), 0.0)).astype(BF16)
        f = f + _dot(t, wdn_ref[cols, :])
    h2 = h1 + _rms(f, g_ref[2:3, :])
    e = _rms(_dot(p_ref[...].astype(BF16), wproj_ref[...]), g_ref[3:4, :])
    gate = jax.nn.sigmoid(_dot(h2.astype(BF16), wgate_ref[...]))
    o_ref[...] = h2 + e * gate


def _post(a, h, p, gains, w_out, w_up, w_down, w_proj, w_gate, *, tm):
    T, D = h.shape
    row = lambda i: (i, 0)
    return pl.pallas_call(
        functools.partial(_post_kernel, ff_chunk=1024),
        grid=(T // tm,),
        in_specs=[
            pl.BlockSpec((tm, a.shape[1]), row),
            pl.BlockSpec((tm, D), row),
            pl.BlockSpec((tm, p.shape[1]), row),
            _resident(gains.shape),
            _resident(w_out.shape),
            _resident(w_up.shape),
            _resident(w_down.shape),
            _resident(w_proj.shape),
            _resident(w_gate.shape),
        ],
        out_specs=pl.BlockSpec((tm, D), row),
        out_shape=jax.ShapeDtypeStruct((T, D), F32),
        compiler_params=_params(("parallel",)),
        name="post",
    )(a, h, p, gains, w_out, w_up, w_down, w_proj, w_gate)


def _pad_rows(x, rows):
    return jnp.concatenate([x, jnp.zeros((rows - x.shape[0],) + x.shape[1:], x.dtype)], axis=0)


def _post_layer(a, h, p, layer, ln_mix_post, ln_mlp_pre, ln_mlp_post, ple_norm, w_out,
                mlp_w_up, mlp_w_down, ple_w_proj, ple_w_gate, *, tm):
    gains = _pad_rows(jnp.stack([ln_mix_post[layer], ln_mlp_pre[layer], ln_mlp_post[layer],
                                 ple_norm[layer]]), SUBLANES)
    return _post(a, h, p, gains, w_out.astype(BF16), mlp_w_up[layer].astype(BF16),
                 mlp_w_down[layer].astype(BF16), ple_w_proj[layer].astype(BF16),
                 ple_w_gate[layer].astype(BF16), tm=tm)


def kernel(x, p, ln_mix_pre, ln_mix_post, ln_mlp_pre, ln_mlp_post, mlp_w_up, mlp_w_down, ple_w_proj, ple_norm, ple_w_gate, attn_w_in, attn_b_forget, attn_w_out, diff_lambda_q1, diff_lambda_k1, diff_lambda_q2, diff_lambda_k2, diff_subln, rec_w_in, rec_conv_w, rec_conv_b, rec_wx, rec_bx, rec_wa, rec_ba, rec_a_param, rec_w_out):
    B, S, D = x.shape
    T = B * S
    depth = p.shape[0]
    tm = 512
    n_diff_heads = D // (4 * HEAD_DIM)
    n_fox_heads = D // (2 * HEAD_DIM)
    n_units = n_diff_heads + n_fox_heads // 2
    qkv_width = 3 * (n_diff_heads + n_fox_heads // 2) * UNIT
    assert S % tm == 0 and attn_w_in.shape[2] == qkv_width + n_fox_heads == qkv_width + SUBLANES

    h = x.reshape(T, D)
    for layer in range(depth):
        j = layer // 2
        gain = ln_mix_pre[layer].reshape(1, D)
        if layer % 2 == 0:
            w_in = attn_w_in[j]
            col = jnp.arange(qkv_width)
            is_q = (col < n_diff_heads * UNIT) | ((col >= 3 * n_diff_heads * UNIT)
                                                  & (col < 3 * n_diff_heads * UNIT + n_fox_heads * HEAD_DIM))
            w_main = (w_in[:, :qkv_width] * jnp.where(is_q, LOG2_E * HEAD_DIM ** -0.5, 1.0)).astype(BF16)
            w_fz_t = _pad_rows(w_in[:, qkv_width:].T, 2 * SUBLANES).astype(BF16)
            z, lf_t = _attn_in(h, gain, w_main, w_fz_t, attn_b_forget[j].reshape(n_fox_heads, 1), tm=tm)
            cb = _fox_bias(lf_t, B=B, S=S, n_diff_heads=n_diff_heads)
            lam_params = _pad_rows(jnp.pad(
                jnp.stack([diff_lambda_q1[j], diff_lambda_k1[j], diff_lambda_q2[j], diff_lambda_k2[j]]),
                ((0, 0), (0, LANES - HEAD_DIM))), SUBLANES)
            lam_init = 0.8 - 0.6 * math.exp(-0.3 * layer)
            a = _attention(z, cb, lam_params, diff_subln[j].reshape(1, UNIT), B=B, S=S, tq=tm,
                           n_diff_units=n_diff_heads, n_units=n_units, lam_init=lam_init)
            w_out = attn_w_out[j]
        else:
            width = rec_w_in.shape[2] // 2
            vecs = jnp.concatenate([rec_conv_w[j], rec_conv_b[j][None], rec_bx[j][None],
                                    rec_ba[j][None], rec_a_param[j][None]], axis=0)
            wxa = jnp.concatenate([rec_wx[j], rec_wa[j]], axis=-1).astype(BF16)
            a = _rec_mixer(h, gain, rec_w_in[j].astype(BF16), vecs, wxa, B=B, S=S, ts=tm)
            w_out = rec_w_out[j]
        h = _post_layer(a, h, p[layer].reshape(T, -1), layer, ln_mix_post, ln_mlp_pre, ln_mlp_post,
                        ple_norm, w_out, mlp_w_up, mlp_w_down, ple_w_proj, ple_w_gate, tm=tm)
    return h.reshape(B, S, D)
```

```python
import functools
import math

import jax
import jax.numpy as jnp
from jax import lax
from jax.experimental import pallas as pl
from jax.experimental.pallas import tpu as pltpu

F32 = jnp.float32
BF16 = jnp.bfloat16

LANES = 128
SUBLANES = 8
VMEM_LIMIT_BYTES = 56 * 1024 * 1024

HEAD_DIM = 64
UNIT = 2 * HEAD_DIM
NORM_EPS = 1e-6
SUBLN_EPS = 1e-5
RG_C = 8.0
CONV_WIDTH = 4
RNN_BLOCK_W = 128
MASKED = -1e30
LOG2_E = math.log2(math.e)


def _rms(x, gain, eps=NORM_EPS):
    return x * lax.rsqrt(jnp.mean(x * x, axis=-1, keepdims=True) + eps) * gain


def _dot(a, b):
    return jnp.dot(a, b, preferred_element_type=F32)


def _dot_nt(a, b):
    return lax.dot_general(a, b, (((1,), (1,)), ((), ())), preferred_element_type=F32)


def _log_sigmoid(x):
    return jnp.minimum(x, 0.0) - jnp.log1p(jnp.exp(-jnp.abs(x)))


def _resident(shape):
    zeros = (0,) * len(shape)
    return pl.BlockSpec(shape, lambda *_: zeros, pipeline_mode=pl.Buffered(1))


def _params(semantics):
    return pltpu.CompilerParams(dimension_semantics=semantics, vmem_limit_bytes=VMEM_LIMIT_BYTES)


def _attn_in_kernel(h_ref, g_ref, w_ref, wfz_ref, bf_ref, z_ref, lf_ref, *, n_chunk):
    hn = _rms(h_ref[...], g_ref[...]).astype(BF16)
    width = w_ref.shape[1] // n_chunk
    for c in range(n_chunk):
        cols = slice(c * width, (c + 1) * width)
        z_ref[:, cols] = _dot(hn, w_ref[:, cols]).astype(BF16)
    fz = _dot_nt(wfz_ref[...], hn)
    lf_ref[...] = _log_sigmoid(fz[:SUBLANES] + bf_ref[...])


def _attn_in(h, gain, w_main, w_fz_t, b_forget, *, tm):
    T, D = h.shape
    N = w_main.shape[1]
    return pl.pallas_call(
        functools.partial(_attn_in_kernel, n_chunk=N // 1024),
        grid=(T // tm,),
        in_specs=[
            pl.BlockSpec((tm, D), lambda i: (i, 0)),
            _resident((1, D)),
            _resident((D, N)),
            _resident(w_fz_t.shape),
            _resident(b_forget.shape),
        ],
        out_specs=[
            pl.BlockSpec((tm, N), lambda i: (i, 0)),
            pl.BlockSpec((SUBLANES, tm), lambda i: (0, i)),
        ],
        out_shape=[
            jax.ShapeDtypeStruct((T, N), BF16),
            jax.ShapeDtypeStruct((SUBLANES, T), F32),
        ],
        compiler_params=_params(("parallel",)),
        name="attn_in",
    )(h, gain, w_main, w_fz_t, b_forget)


def _fox_bias_kernel(lf_ref, cb_ref, *, n_diff_heads):
    x = lf_ref[...]
    S = x.shape[1]
    pos = lax.broadcasted_iota(jnp.int32, x.shape, 1)
    k = 1
    while k < S:
        x = x + jnp.where(pos >= k, pltpu.roll(x, k, axis=1), 0.0)
        k *= 2
    row = lax.broadcasted_iota(jnp.int32, x.shape, 0)
    head = (row // 2 + 1).astype(F32)
    slope = jnp.exp2(-8.0 * head / n_diff_heads)
    cb_ref[0:SUBLANES, :] = (LOG2_E * slope) * pos.astype(F32)
    cb_ref[SUBLANES:, :] = -LOG2_E * x


def _fox_bias(lf_t, *, B, S, n_diff_heads):
    return pl.pallas_call(
        functools.partial(_fox_bias_kernel, n_diff_heads=n_diff_heads),
        grid=(B,),
        in_specs=[pl.BlockSpec((SUBLANES, S), lambda b: (0, b))],
        out_specs=pl.BlockSpec((None, 2 * SUBLANES, S), lambda b: (b, 0, 0)),
        out_shape=jax.ShapeDtypeStruct((B, 2 * SUBLANES, S), F32),
        compiler_params=_params(("parallel",)),
        name="fox_bias",
    )(lf_t)


def _attention_kernel(q_ref, k_ref, v_ref, cb_ref, lp_ref, sg_ref, o_ref, m_ref, acc_ref,
                      *, tq, ups, n_diff_groups, lam_init):
    g = pl.program_id(1)
    qi = pl.program_id(2)
    q = q_ref[...]
    lane = lax.broadcasted_iota(jnp.int32, (tq, UNIT), 1)
    zero = jnp.zeros((tq, UNIT), BF16)
    q_stream = []
    for j in range(ups):
        qu = q[:, j * UNIT:(j + 1) * UNIT]
        q_stream += [jnp.where(lane < HEAD_DIM, qu, zero), jnp.where(lane >= HEAD_DIM, qu, zero)]

    m_ref[...] = jnp.full(m_ref.shape, MASKED, F32)
    acc_ref[...] = jnp.zeros(acc_ref.shape, F32)

    def block(ki, diagonal):
        start = pl.multiple_of(ki * tq, tq)
        for j in range(ups):
            k = k_ref[pl.ds(start, tq), j * UNIT:(j + 1) * UNIT]
            v = v_ref[pl.ds(start, tq), j * UNIT:(j + 1) * UNIT]
            v_ones = jnp.concatenate([v, jnp.ones_like(v)], axis=1)
            for s in (2 * j, 2 * j + 1):
                sc = _dot_nt(q_stream[s], k) + cb_ref[s:s + 1, pl.ds(start, tq)]
                if diagonal:
                    r = lax.broadcasted_iota(jnp.int32, sc.shape, 0)
                    c = lax.broadcasted_iota(jnp.int32, sc.shape, 1)
                    sc = jnp.where(c <= r, sc, MASKED)
                m_old = m_ref[s]
                m_new = jnp.maximum(m_old, jnp.max(sc, axis=-1, keepdims=True))
                alpha = jnp.exp2(m_old - m_new)
                p = jnp.exp2(sc - jnp.tile(m_new, (1, tq // LANES)))
                acc_ref[s] = jnp.tile(alpha, (1, 2)) * acc_ref[s] + _dot(p.astype(BF16), v_ones)
                m_ref[s] = m_new

    def body(ki, carry):
        block(ki, diagonal=False)
        return carry

    lax.fori_loop(0, qi, body, 0)
    block(qi, diagonal=True)

    outs = [acc_ref[s, :, :UNIT] / acc_ref[s, :, UNIT:] for s in range(2 * ups)]

    @pl.when(g < n_diff_groups)
    def _():
        lp = lp_ref[...]
        lam = (jnp.exp(jnp.sum(lp[0:1] * lp[1:2], axis=-1, keepdims=True))
               - jnp.exp(jnp.sum(lp[2:3] * lp[3:4], axis=-1, keepdims=True)) + lam_init)
        for j in range(ups):
            d = outs[2 * j] - lam * outs[2 * j + 1]
            o_ref[:, j * UNIT:(j + 1) * UNIT] = (
                _rms(d, sg_ref[...], SUBLN_EPS) * (1.0 - lam_init)).astype(o_ref.dtype)

    @pl.when(g >= n_diff_groups)
    def _():
        for j in range(ups):
            o_ref[:, j * UNIT:(j + 1) * UNIT] = jnp.where(
                lane < HEAD_DIM, outs[2 * j], outs[2 * j + 1]).astype(o_ref.dtype)


def _attention(z, cb, lam_params, subln, *, B, S, tq, ups, n_diff_units, n_units, lam_init):
    T = B * S
    nq = S // tq
    n_groups = n_units // ups
    n_diff_groups = n_diff_units // ups
    n_fox_groups = n_groups - n_diff_groups
    def qcol(g):
        return jnp.where(g < n_diff_groups, g, g + 2 * n_diff_groups)

    def kcol(g):
        return jnp.where(g < n_diff_groups, g + n_diff_groups, g + 2 * n_diff_groups + n_fox_groups)

    def vcol(g):
        return jnp.where(g < n_diff_groups, g + 2 * n_diff_groups, g + 2 * n_diff_groups + 2 * n_fox_groups)

    cb4 = cb.reshape(B, n_groups, 2 * ups, S)
    return pl.pallas_call(
        functools.partial(_attention_kernel, tq=tq, ups=ups, n_diff_groups=n_diff_groups, lam_init=lam_init),
        grid=(B, n_groups, nq),
        in_specs=[
            pl.BlockSpec((tq, ups * UNIT), lambda b, g, i: (b * nq + i, qcol(g))),
            pl.BlockSpec((S, ups * UNIT), lambda b, g, i: (b, kcol(g))),
            pl.BlockSpec((S, ups * UNIT), lambda b, g, i: (b, vcol(g))),
            pl.BlockSpec((None, None, 2 * ups, S), lambda b, g, i: (b, g, 0, 0)),
            _resident(lam_params.shape),
            _resident(subln.shape),
        ],
        out_specs=pl.BlockSpec((tq, ups * UNIT), lambda b, g, i: (b * nq + i, g)),
        out_shape=jax.ShapeDtypeStruct((T, n_units * UNIT), BF16),
        scratch_shapes=[
            pltpu.VMEM((2 * ups, tq, LANES), F32),
            pltpu.VMEM((2 * ups, tq, 2 * UNIT), F32),
        ],
        compiler_params=_params(("parallel", "parallel", "arbitrary")),
        name="attention",
    )(z, z, z, cb4, lam_params, subln)


def _gelu_tanh(x):
    c = 2.0 * math.sqrt(2.0 / math.pi)
    return x * jax.nn.sigmoid(x * (c + (c * 0.044715) * (x * x)))


def _rec_kernel(h_ref, g_ref, win_ref, vec_ref, wxa_ref, o_ref, xbuf, a_buf, b_buf, hstate, *, ts, width):
    si = pl.program_id(1)
    n_blocks = width // RNN_BLOCK_W

    @pl.when(si == 0)
    def _():
        xbuf[0:SUBLANES, :] = jnp.zeros((SUBLANES, width), F32)
        hstate[...] = jnp.zeros(hstate.shape, F32)

    hn = _rms(h_ref[...], g_ref[...]).astype(BF16)
    y = _gelu_tanh(_dot(hn, win_ref[:, :width]))
    xbuf[SUBLANES:, :] = _dot(hn, win_ref[:, width:])

    xc = vec_ref[CONV_WIDTH:CONV_WIDTH + 1, :]
    for w in range(CONV_WIDTH):
        off = SUBLANES - (CONV_WIDTH - 1) + w
        xc = xc + vec_ref[w:w + 1, :] * xbuf[off:off + ts, :]
    xbuf[0:SUBLANES, :] = xbuf[ts:ts + SUBLANES, :]

    log_sig_a = _log_sigmoid(vec_ref[7:8, :])
    row = lax.broadcasted_iota(jnp.int32, (ts, RNN_BLOCK_W), 0)
    first = jnp.logical_and(row == 0, si == 0)
    for n in range(n_blocks):
        cols = slice(n * RNN_BLOCK_W, (n + 1) * RNN_BLOCK_W)
        xcn = xc[:, cols]
        g = _dot(xcn.astype(BF16), wxa_ref[n])
        gate_x = jax.nn.sigmoid(g[:, :RNN_BLOCK_W] + vec_ref[5:6, cols])
        gate_a = jax.nn.sigmoid(g[:, RNN_BLOCK_W:] + vec_ref[6:7, cols])
        log_a = RG_C * gate_a * log_sig_a[:, cols]
        a_n = jnp.exp(log_a)
        mult = jnp.sqrt(1.0 - a_n * a_n)
        mult = jnp.where(first, 1.0, mult)
        a_buf[:, cols] = a_n
        b_buf[:, cols] = mult * gate_x * xcn

    a = a_buf[...]
    b = b_buf[...]
    sub = lax.broadcasted_iota(jnp.int32, a.shape, 0) % SUBLANES
    k = 1
    while k < SUBLANES:
        keep = sub >= k
        b = jnp.where(keep, a * pltpu.roll(b, k, axis=0) + b, b)
        a = jnp.where(keep, a * pltpu.roll(a, k, axis=0), a)
        k *= 2
    a_buf[...] = a
    b_buf[...] = b

    carry = hstate[...]
    for gidx in range(ts // SUBLANES):
        rows = slice(gidx * SUBLANES, (gidx + 1) * SUBLANES)
        hg = a_buf[rows, :] * carry + b_buf[rows, :]
        o_ref[rows, :] = (hg * y[rows, :]).astype(o_ref.dtype)
        carry = hg[SUBLANES - 1:SUBLANES, :]
    hstate[...] = carry


def _rec_mixer(h, gain, w_in, vecs, wxa, *, B, S, ts):
    T, D = h.shape
    width = w_in.shape[1] // 2
    ns = S // ts
    return pl.pallas_call(
        functools.partial(_rec_kernel, ts=ts, width=width),
        grid=(B, ns),
        in_specs=[
            pl.BlockSpec((ts, D), lambda b, s: (b * ns + s, 0)),
            _resident((1, D)),
            _resident(w_in.shape),
            _resident(vecs.shape),
            _resident(wxa.shape),
        ],
        out_specs=pl.BlockSpec((ts, width), lambda b, s: (b * ns + s, 0)),
        out_shape=jax.ShapeDtypeStruct((T, width), BF16),
        scratch_shapes=[
            pltpu.VMEM((ts + SUBLANES, width), F32),
            pltpu.VMEM((ts, width), F32),
            pltpu.VMEM((ts, width), F32),
            pltpu.VMEM((1, width), F32),
        ],
        compiler_params=_params(("arbitrary", "arbitrary")),
        name="rec_mixer",
    )(h, gain, w_in, vecs, wxa)


def _post_kernel(a_ref, h_ref, p_ref, g_ref, wout_ref, wup_ref, wdn_ref, wproj_ref, wgate_ref, o_ref,
                 *, ff_chunk):
    m = _dot(a_ref[...], wout_ref[...])
    h1 = h_ref[...] + _rms(m, g_ref[0:1, :])
    u = _rms(h1, g_ref[1:2, :]).astype(BF16)
    f = jnp.zeros_like(h1)
    for c in range(wup_ref.shape[1] // ff_chunk):
        cols = slice(c * ff_chunk, (c + 1) * ff_chunk)
        t = jnp.square(jnp.maximum(_dot(u, wup_ref[:, cols]), 0.0)).astype(BF16)
        f = f + _dot(t, wdn_ref[cols, :])
    h2 = h1 + _rms(f, g_ref[2:3, :])
    e = _rms(_dot(p_ref[...].astype(BF16), wproj_ref[...]), g_ref[3:4, :])
    gate = jax.nn.sigmoid(_dot(h2.astype(BF16), wgate_ref[...]))
    o_ref[...] = h2 + e * gate


def _post(a, h, p, gains, w_out, w_up, w_down, w_proj, w_gate, *, tm):
    T, D = h.shape
    row = lambda i: (i, 0)
    return pl.pallas_call(
        functools.partial(_post_kernel, ff_chunk=1024),
        grid=(T // tm,),
        in_specs=[
            pl.BlockSpec((tm, a.shape[1]), row),
            pl.BlockSpec((tm, D), row),
            pl.BlockSpec((tm, p.shape[1]), row),
            _resident(gains.shape),
            _resident(w_out.shape),
            _resident(w_up.shape),
            _resident(w_down.shape),
            _resident(w_proj.shape),
            _resident(w_gate.shape),
        ],
        out_specs=pl.BlockSpec((tm, D), row),
        out_shape=jax.ShapeDtypeStruct((T, D), F32),
        compiler_params=_params(("parallel",)),
        name="post",
    )(a, h, p, gains, w_out, w_up, w_down, w_proj, w_gate)


def _pad_rows(x, rows):
    return jnp.concatenate([x, jnp.zeros((rows - x.shape[0],) + x.shape[1:], x.dtype)], axis=0)


def _post_layer(a, h, p, layer, ln_mix_post, ln_mlp_pre, ln_mlp_post, ple_norm, w_out,
                mlp_w_up, mlp_w_down, ple_w_proj, ple_w_gate, *, tm):
    gains = _pad_rows(jnp.stack([ln_mix_post[layer], ln_mlp_pre[layer], ln_mlp_post[layer],
                                 ple_norm[layer]]), SUBLANES)
    return _post(a, h, p, gains, w_out.astype(BF16), mlp_w_up[layer].astype(BF16),
                 mlp_w_down[layer].astype(BF16), ple_w_proj[layer].astype(BF16),
                 ple_w_gate[layer].astype(BF16), tm=tm)


def kernel(x, p, ln_mix_pre, ln_mix_post, ln_mlp_pre, ln_mlp_post, mlp_w_up, mlp_w_down, ple_w_proj, ple_norm, ple_w_gate, attn_w_in, attn_b_forget, attn_w_out, diff_lambda_q1, diff_lambda_k1, diff_lambda_q2, diff_lambda_k2, diff_subln, rec_w_in, rec_conv_w, rec_conv_b, rec_wx, rec_bx, rec_wa, rec_ba, rec_a_param, rec_w_out):
    B, S, D = x.shape
    T = B * S
    depth = p.shape[0]
    tm = 512
    n_diff_heads = D // (4 * HEAD_DIM)
    n_fox_heads = D // (2 * HEAD_DIM)
    n_units = n_diff_heads + n_fox_heads // 2
    qkv_width = 3 * (n_diff_heads + n_fox_heads // 2) * UNIT
    assert S % tm == 0 and attn_w_in.shape[2] == qkv_width + n_fox_heads == qkv_width + SUBLANES

    h = x.reshape(T, D)
    for layer in range(depth):
        j = layer // 2
        gain = ln_mix_pre[layer].reshape(1, D)
        if layer % 2 == 0:
            w_in = attn_w_in[j]
            col = jnp.arange(qkv_width)
            is_q = (col < n_diff_heads * UNIT) | ((col >= 3 * n_diff_heads * UNIT)
                                                  & (col < 3 * n_diff_heads * UNIT + n_fox_heads * HEAD_DIM))
            w_main = (w_in[:, :qkv_width] * jnp.where(is_q, LOG2_E * HEAD_DIM ** -0.5, 1.0)).astype(BF16)
            w_fz_t = _pad_rows(w_in[:, qkv_width:].T, 2 * SUBLANES).astype(BF16)
            z, lf_t = _attn_in(h, gain, w_main, w_fz_t, attn_b_forget[j].reshape(n_fox_heads, 1), tm=tm)
            cb = _fox_bias(lf_t, B=B, S=S, n_diff_heads=n_diff_heads)
            lam_params = _pad_rows(jnp.pad(
                jnp.stack([diff_lambda_q1[j], diff_lambda_k1[j], diff_lambda_q2[j], diff_lambda_k2[j]]),
                ((0, 0), (0, LANES - HEAD_DIM))), SUBLANES)
            lam_init = 0.8 - 0.6 * math.exp(-0.3 * layer)
            a = _attention(z, cb, lam_params, diff_subln[j].reshape(1, UNIT), B=B, S=S, tq=tm, ups=2,
                           n_diff_units=n_diff_heads, n_units=n_units, lam_init=lam_init)
            w_out = attn_w_out[j]
        else:
            width = rec_w_in.shape[2] // 2
            vecs = jnp.concatenate([rec_conv_w[j], rec_conv_b[j][None], rec_bx[j][None],
                                    rec_ba[j][None], rec_a_param[j][None]], axis=0)
            wxa = jnp.concatenate([rec_wx[j], rec_wa[j]], axis=-1).astype(BF16)
            a = _rec_mixer(h, gain, rec_w_in[j].astype(BF16), vecs, wxa, B=B, S=S, ts=tm)
            w_out = rec_w_out[j]
        h = _post_layer(a, h, p[layer].reshape(T, -1), layer, ln_mix_post, ln_mlp_pre, ln_mlp_post,
                        ple_norm, w_out, mlp_w_up, mlp_w_down, ple_w_proj, ple_w_gate, tm=tm)
    return h.reshape(B, S, D)
```

```python
import functools
import math

import jax
import jax.numpy as jnp
from jax import lax
from jax.experimental import pallas as pl
from jax.experimental.pallas import tpu as pltpu

F32 = jnp.float32
BF16 = jnp.bfloat16

LANES = 128
SUBLANES = 8
VMEM_LIMIT_BYTES = 56 * 1024 * 1024

HEAD_DIM = 64
UNIT = 2 * HEAD_DIM
NORM_EPS = 1e-6
SUBLN_EPS = 1e-5
RG_C = 8.0
CONV_WIDTH = 4
RNN_BLOCK_W = 128
MASKED = -1e30
LOG2_E = math.log2(math.e)


def _rms(x, gain, eps=NORM_EPS):
    return x * lax.rsqrt(jnp.mean(x * x, axis=-1, keepdims=True) + eps) * gain


def _dot(a, b):
    return jnp.dot(a, b, preferred_element_type=F32)


def _dot_nt(a, b):
    return lax.dot_general(a, b, (((1,), (1,)), ((), ())), preferred_element_type=F32)


def _log_sigmoid(x):
    return jnp.minimum(x, 0.0) - jnp.log1p(jnp.exp(-jnp.abs(x)))


def _resident(shape):
    zeros = (0,) * len(shape)
    return pl.BlockSpec(shape, lambda *_: zeros, pipeline_mode=pl.Buffered(1))


def _params(semantics):
    return pltpu.CompilerParams(dimension_semantics=semantics, vmem_limit_bytes=VMEM_LIMIT_BYTES)


def _attn_in_kernel(h_ref, g_ref, w_ref, wfz_ref, bf_ref, z_ref, lf_ref, *, n_chunk):
    hn = _rms(h_ref[...], g_ref[...]).astype(BF16)
    width = w_ref.shape[1] // n_chunk
    for c in range(n_chunk):
        cols = slice(c * width, (c + 1) * width)
        z_ref[:, cols] = _dot(hn, w_ref[:, cols]).astype(BF16)
    fz = _dot_nt(wfz_ref[...], hn)
    lf_ref[...] = _log_sigmoid(fz[:SUBLANES] + bf_ref[...])


def _attn_in(h, gain, w_main, w_fz_t, b_forget, *, tm):
    T, D = h.shape
    N = w_main.shape[1]
    return pl.pallas_call(
        functools.partial(_attn_in_kernel, n_chunk=N // 1024),
        grid=(T // tm,),
        in_specs=[
            pl.BlockSpec((tm, D), lambda i: (i, 0)),
            _resident((1, D)),
            _resident((D, N)),
            _resident(w_fz_t.shape),
            _resident(b_forget.shape),
        ],
        out_specs=[
            pl.BlockSpec((tm, N), lambda i: (i, 0)),
            pl.BlockSpec((SUBLANES, tm), lambda i: (0, i)),
        ],
        out_shape=[
            jax.ShapeDtypeStruct((T, N), BF16),
            jax.ShapeDtypeStruct((SUBLANES, T), F32),
        ],
        compiler_params=_params(("parallel",)),
        name="attn_in",
    )(h, gain, w_main, w_fz_t, b_forget)


def _fox_bias_kernel(lf_ref, cb_ref, *, n_diff_heads):
    x = lf_ref[...]
    S = x.shape[1]
    pos = lax.broadcasted_iota(jnp.int32, x.shape, 1)
    k = 1
    while k < S:
        x = x + jnp.where(pos >= k, pltpu.roll(x, k, axis=1), 0.0)
        k *= 2
    row = lax.broadcasted_iota(jnp.int32, x.shape, 0)
    head = (row // 2 + 1).astype(F32)
    slope = jnp.exp2(-8.0 * head / n_diff_heads)
    cb_ref[0:SUBLANES, :] = (LOG2_E * slope) * pos.astype(F32)
    cb_ref[SUBLANES:, :] = -LOG2_E * x


def _fox_bias(lf_t, *, B, S, n_diff_heads):
    return pl.pallas_call(
        functools.partial(_fox_bias_kernel, n_diff_heads=n_diff_heads),
        grid=(B,),
        in_specs=[pl.BlockSpec((SUBLANES, S), lambda b: (0, b))],
        out_specs=pl.BlockSpec((None, 2 * SUBLANES, S), lambda b: (b, 0, 0)),
        out_shape=jax.ShapeDtypeStruct((B, 2 * SUBLANES, S), F32),
        compiler_params=_params(("parallel",)),
        name="fox_bias",
    )(lf_t)


def _attention_kernel(q_ref, k_ref, v_ref, cb_ref, lp_ref, sg_ref, o_ref, qs_ref, m_ref, acc_ref, sc_a, sc_b,
                      *, tq, ups, n_diff_groups, lam_init):
    g = pl.program_id(1)
    qi = pl.program_id(2)
    lane = lax.broadcasted_iota(jnp.int32, (tq, UNIT), 1)
    zero = jnp.zeros((tq, UNIT), BF16)
    for j in range(ups):
        qu = q_ref[:, j * UNIT:(j + 1) * UNIT]
        qs_ref[2 * j] = jnp.where(lane < HEAD_DIM, qu, zero)
        qs_ref[2 * j + 1] = jnp.where(lane >= HEAD_DIM, qu, zero)
    m_ref[...] = jnp.full(m_ref.shape, MASKED, F32)
    acc_ref[...] = jnp.zeros(acc_ref.shape, F32)

    def scores(ki, sc_ref):
        start = pl.multiple_of(ki * tq, tq)
        for j in range(ups):
            k = k_ref[pl.ds(start, tq), j * UNIT:(j + 1) * UNIT]
            for s in (2 * j, 2 * j + 1):
                sc_ref[s] = _dot_nt(qs_ref[s], k) + cb_ref[s:s + 1, pl.ds(start, tq)]

    def softmax_pv(ki, sc_ref, masked):
        start = pl.multiple_of(ki * tq, tq)
        if masked:
            r = lax.broadcasted_iota(jnp.int32, (tq, tq), 0)
            c = lax.broadcasted_iota(jnp.int32, (tq, tq), 1)
            visible = c - r <= (qi - ki) * tq
        for j in range(ups):
            v = v_ref[pl.ds(start, tq), j * UNIT:(j + 1) * UNIT]
            v_ones = jnp.concatenate([v, jnp.ones_like(v)], axis=1)
            for s in (2 * j, 2 * j + 1):
                sc = sc_ref[s]
                if masked:
                    sc = jnp.where(visible, sc, MASKED)
                m_old = m_ref[s]
                m_new = jnp.maximum(m_old, jnp.max(sc, axis=-1, keepdims=True))
                alpha = jnp.exp2(m_old - m_new)
                p = jnp.exp2(sc - jnp.tile(m_new, (1, tq // LANES)))
                acc_ref[s] = jnp.tile(alpha, (1, 2)) * acc_ref[s] + _dot(p.astype(BF16), v_ones)
                m_ref[s] = m_new

    scores(0, sc_a)

    def pair(t, carry):
        scores(2 * t + 1, sc_b)
        softmax_pv(2 * t, sc_a, masked=False)
        scores(2 * t + 2, sc_a)
        softmax_pv(2 * t + 1, sc_b, masked=False)
        return carry

    n_pairs = qi // 2
    lax.fori_loop(0, n_pairs, pair, 0)
    odd = qi % 2 == 1

    @pl.when(odd)
    def _():
        scores(qi, sc_b)

    softmax_pv(2 * n_pairs, sc_a, masked=True)

    @pl.when(odd)
    def _():
        softmax_pv(qi, sc_b, masked=True)

    outs = [acc_ref[s, :, :UNIT] / acc_ref[s, :, UNIT:] for s in range(2 * ups)]

    @pl.when(g < n_diff_groups)
    def _():
        lp = lp_ref[...]
        lam = (jnp.exp(jnp.sum(lp[0:1] * lp[1:2], axis=-1, keepdims=True))
               - jnp.exp(jnp.sum(lp[2:3] * lp[3:4], axis=-1, keepdims=True)) + lam_init)
        for j in range(ups):
            d = outs[2 * j] - lam * outs[2 * j + 1]
            o_ref[:, j * UNIT:(j + 1) * UNIT] = (
                _rms(d, sg_ref[...], SUBLN_EPS) * (1.0 - lam_init)).astype(o_ref.dtype)

    @pl.when(g >= n_diff_groups)
    def _():
        for j in range(ups):
            o_ref[:, j * UNIT:(j + 1) * UNIT] = jnp.where(
                lane < HEAD_DIM, outs[2 * j], outs[2 * j + 1]).astype(o_ref.dtype)


def _attention(z, cb, lam_params, subln, *, B, S, tq, ups, n_diff_units, n_units, lam_init):
    T = B * S
    nq = S // tq
    n_groups = n_units // ups
    n_diff_groups = n_diff_units // ups
    n_fox_groups = n_groups - n_diff_groups
    def qcol(g):
        return jnp.where(g < n_diff_groups, g, g + 2 * n_diff_groups)

    def kcol(g):
        return jnp.where(g < n_diff_groups, g + n_diff_groups, g + 2 * n_diff_groups + n_fox_groups)

    def vcol(g):
        return jnp.where(g < n_diff_groups, g + 2 * n_diff_groups, g + 2 * n_diff_groups + 2 * n_fox_groups)

    cb4 = cb.reshape(B, n_groups, 2 * ups, S)
    return pl.pallas_call(
        functools.partial(_attention_kernel, tq=tq, ups=ups, n_diff_groups=n_diff_groups, lam_init=lam_init),
        grid=(B, n_groups, nq),
        in_specs=[
            pl.BlockSpec((tq, ups * UNIT), lambda b, g, i: (b * nq + i, qcol(g))),
            pl.BlockSpec((S, ups * UNIT), lambda b, g, i: (b, kcol(g))),
            pl.BlockSpec((S, ups * UNIT), lambda b, g, i: (b, vcol(g))),
            pl.BlockSpec((None, None, 2 * ups, S), lambda b, g, i: (b, g, 0, 0)),
            _resident(lam_params.shape),
            _resident(subln.shape),
        ],
        out_specs=pl.BlockSpec((tq, ups * UNIT), lambda b, g, i: (b * nq + i, g)),
        out_shape=jax.ShapeDtypeStruct((T, n_units * UNIT), BF16),
        scratch_shapes=[
            pltpu.VMEM((2 * ups, tq, UNIT), BF16),
            pltpu.VMEM((2 * ups, tq, LANES), F32),
            pltpu.VMEM((2 * ups, tq, 2 * UNIT), F32),
            pltpu.VMEM((2 * ups, tq, tq), F32),
            pltpu.VMEM((2 * ups, tq, tq), F32),
        ],
        compiler_params=_params(("parallel", "parallel", "arbitrary")),
        name="attention",
    )(z, z, z, cb4, lam_params, subln)


def _gelu_tanh(x):
    c = 2.0 * math.sqrt(2.0 / math.pi)
    return x * jax.nn.sigmoid(x * (c + (c * 0.044715) * (x * x)))


def _rec_kernel(h_ref, g_ref, win_ref, vec_ref, wxa_ref, o_ref, tail_ref, a_buf, b_buf, hstate, *, ts, width):
    si = pl.program_id(1)
    n_blocks = width // RNN_BLOCK_W
    n_groups = ts // SUBLANES

    @pl.when(si == 0)
    def _():
        tail_ref[...] = jnp.zeros(tail_ref.shape, F32)
        hstate[...] = jnp.zeros(hstate.shape, F32)

    hn = _rms(h_ref[...], g_ref[...]).astype(BF16)
    y = _gelu_tanh(_dot(hn, win_ref[:, :width]))
    xr = _dot(hn, win_ref[:, width:]).reshape(n_groups, SUBLANES, width)

    ext = jnp.concatenate([tail_ref[...][None], xr], axis=0)
    tail_ref[...] = xr[n_groups - 1]
    sub = lax.broadcasted_iota(jnp.int32, (1, SUBLANES, RNN_BLOCK_W), 1)
    sub_w = lax.broadcasted_iota(jnp.int32, (1, SUBLANES, width), 1)

    def vec(i):
        return vec_ref[i:i + 1, :].reshape(1, 1, width)

    xc = vec(CONV_WIDTH) + vec(CONV_WIDTH - 1) * xr
    for d in range(1, CONV_WIDTH):
        rot = pltpu.roll(ext, d, axis=1)
        xc = xc + vec(CONV_WIDTH - 1 - d) * jnp.where(sub_w >= d, rot[1:], rot[:-1])
    xc = xc.reshape(ts, width)

    log_sig_a = _log_sigmoid(vec_ref[7:8, :])
    start = jnp.logical_and(sub == 0, si == 0)
    for n in range(n_blocks):
        cols = slice(n * RNN_BLOCK_W, (n + 1) * RNN_BLOCK_W)
        xcn = xc[:, cols]
        g = _dot(xcn.astype(BF16), wxa_ref[n])
        gate_x = jax.nn.sigmoid(g[:, :RNN_BLOCK_W] + vec_ref[5:6, cols])
        gate_a = jax.nn.sigmoid(g[:, RNN_BLOCK_W:] + vec_ref[6:7, cols])
        a = jnp.exp(RG_C * gate_a * log_sig_a[:, cols]).reshape(n_groups, SUBLANES, RNN_BLOCK_W)
        a = jnp.concatenate([jnp.where(start, 0.0, a[:1]), a[1:]], axis=0)
        mult = jnp.sqrt(1.0 - a * a)
        b = mult * (gate_x * xcn).reshape(n_groups, SUBLANES, RNN_BLOCK_W)
        k = 1
        while k < SUBLANES:
            keep = sub >= k
            b = b + jnp.where(keep, a, 0.0) * pltpu.roll(b, k, axis=1)
            a = a * jnp.where(keep, pltpu.roll(a, k, axis=1), 1.0)
            k *= 2
        a_buf[:, :, cols] = a
        b_buf[:, :, cols] = b

    carry = hstate[...]
    for gidx in range(n_groups):
        rows = slice(gidx * SUBLANES, (gidx + 1) * SUBLANES)
        hg = a_buf[gidx] * carry + b_buf[gidx]
        o_ref[rows, :] = (hg * y[rows, :]).astype(o_ref.dtype)
        carry = hg[SUBLANES - 1:SUBLANES, :]
    hstate[...] = carry


def _rec_mixer(h, gain, w_in, vecs, wxa, *, B, S, ts):
    T, D = h.shape
    width = w_in.shape[1] // 2
    ns = S // ts
    return pl.pallas_call(
        functools.partial(_rec_kernel, ts=ts, width=width),
        grid=(B, ns),
        in_specs=[
            pl.BlockSpec((ts, D), lambda b, s: (b * ns + s, 0)),
            _resident((1, D)),
            _resident(w_in.shape),
            _resident(vecs.shape),
            _resident(wxa.shape),
        ],
        out_specs=pl.BlockSpec((ts, width), lambda b, s: (b * ns + s, 0)),
        out_shape=jax.ShapeDtypeStruct((T, width), BF16),
        scratch_shapes=[
            pltpu.VMEM((SUBLANES, width), F32),
            pltpu.VMEM((ts // SUBLANES, SUBLANES, width), F32),
            pltpu.VMEM((ts // SUBLANES, SUBLANES, width), F32),
            pltpu.VMEM((1, width), F32),
        ],
        compiler_params=_params(("arbitrary", "arbitrary")),
        name="rec_mixer",
    )(h, gain, w_in, vecs, wxa)


def _post_kernel(a_ref, h_ref, p_ref, g_ref, wout_ref, wup_ref, wdn_ref, wproj_ref, wgate_ref, o_ref,
                 *, ff_chunk):
    m = _dot(a_ref[...], wout_ref[...])
    h1 = h_ref[...] + _rms(m, g_ref[0:1, :])
    u = _rms(h1, g_ref[1:2, :]).astype(BF16)
    f = jnp.zeros_like(h1)
    for c in range(wup_ref.shape[1] // ff_chunk):
        cols = slice(c * ff_chunk, (c + 1) * ff_chunk)
        t = jnp.square(jnp.maximum(_dot(u, wup_ref[:, cols]), 0.0)).astype(BF16)
        f = f + _dot(t, wdn_ref[cols, :])
    h2 = h1 + _rms(f, g_ref[2:3, :])
    e = _rms(_dot(p_ref[...].astype(BF16), wproj_ref[...]), g_ref[3:4, :])
    gate = jax.nn.sigmoid(_dot(h2.astype(BF16), wgate_ref[...]))
    o_ref[...] = h2 + e * gate


def _post(a, h, p, gains, w_out, w_up, w_down, w_proj, w_gate, *, layer, tm):
    T, D = h.shape
    row = lambda i: (i, 0)
    return pl.pallas_call(
        functools.partial(_post_kernel, ff_chunk=1024),
        grid=(T // tm,),
        in_specs=[
            pl.BlockSpec((tm, a.shape[1]), row),
            pl.BlockSpec((tm, D), row),
            pl.BlockSpec((None, tm, p.shape[2]), lambda i: (layer, i, 0)),
            _resident(gains.shape),
            _resident(w_out.shape),
            _resident(w_up.shape),
            _resident(w_down.shape),
            _resident(w_proj.shape),
            _resident(w_gate.shape),
        ],
        out_specs=pl.BlockSpec((tm, D), row),
        out_shape=jax.ShapeDtypeStruct((T, D), F32),
        compiler_params=_params(("parallel",)),
        name="post",
    )(a, h, p, gains, w_out, w_up, w_down, w_proj, w_gate)


def _pad_rows(x, rows):
    return jnp.concatenate([x, jnp.zeros((rows - x.shape[0],) + x.shape[1:], x.dtype)], axis=0)


def _post_layer(a, h, p, layer, ln_mix_post, ln_mlp_pre, ln_mlp_post, ple_norm, w_out,
                mlp_w_up, mlp_w_down, ple_w_proj, ple_w_gate, *, tm):
    gains = _pad_rows(jnp.stack([ln_mix_post[layer], ln_mlp_pre[layer], ln_mlp_post[layer],
                                 ple_norm[layer]]), SUBLANES)
    return _post(a, h, p, gains, w_out.astype(BF16), mlp_w_up[layer].astype(BF16),
                 mlp_w_down[layer].astype(BF16), ple_w_proj[layer].astype(BF16),
                 ple_w_gate[layer].astype(BF16), layer=layer, tm=tm)


def kernel(x, p, ln_mix_pre, ln_mix_post, ln_mlp_pre, ln_mlp_post, mlp_w_up, mlp_w_down, ple_w_proj, ple_norm, ple_w_gate, attn_w_in, attn_b_forget, attn_w_out, diff_lambda_q1, diff_lambda_k1, diff_lambda_q2, diff_lambda_k2, diff_subln, rec_w_in, rec_conv_w, rec_conv_b, rec_wx, rec_bx, rec_wa, rec_ba, rec_a_param, rec_w_out):
    B, S, D = x.shape
    T = B * S
    depth = p.shape[0]
    tm = 512
    n_diff_heads = D // (4 * HEAD_DIM)
    n_fox_heads = D // (2 * HEAD_DIM)
    n_units = n_diff_heads + n_fox_heads // 2
    qkv_width = 3 * (n_diff_heads + n_fox_heads // 2) * UNIT
    assert S % tm == 0 and attn_w_in.shape[2] == qkv_width + n_fox_heads == qkv_width + SUBLANES

    h = x.reshape(T, D)
    for layer in range(depth):
        j = layer // 2
        gain = ln_mix_pre[layer].reshape(1, D)
        if layer % 2 == 0:
            w_in = attn_w_in[j]
            col = jnp.arange(qkv_width)
            is_q = (col < n_diff_heads * UNIT) | ((col >= 3 * n_diff_heads * UNIT)
                                                  & (col < 3 * n_diff_heads * UNIT + n_fox_heads * HEAD_DIM))
            w_main = (w_in[:, :qkv_width] * jnp.where(is_q, LOG2_E * HEAD_DIM ** -0.5, 1.0)).astype(BF16)
            w_fz_t = _pad_rows(w_in[:, qkv_width:].T, 2 * SUBLANES).astype(BF16)
            z, lf_t = _attn_in(h, gain, w_main, w_fz_t, attn_b_forget[j].reshape(n_fox_heads, 1), tm=tm)
            cb = _fox_bias(lf_t, B=B, S=S, n_diff_heads=n_diff_heads)
            lam_params = _pad_rows(jnp.pad(
                jnp.stack([diff_lambda_q1[j], diff_lambda_k1[j], diff_lambda_q2[j], diff_lambda_k2[j]]),
                ((0, 0), (0, LANES - HEAD_DIM))), SUBLANES)
            lam_init = 0.8 - 0.6 * math.exp(-0.3 * layer)
            a = _attention(z, cb, lam_params, diff_subln[j].reshape(1, UNIT), B=B, S=S, tq=tm, ups=2,
                           n_diff_units=n_diff_heads, n_units=n_units, lam_init=lam_init)
            w_out = attn_w_out[j]
        else:
            width = rec_w_in.shape[2] // 2
            vecs = jnp.concatenate([rec_conv_w[j], rec_conv_b[j][None], rec_bx[j][None],
                                    rec_ba[j][None], rec_a_param[j][None]], axis=0)
            wxa = jnp.concatenate([rec_wx[j], rec_wa[j]], axis=-1).astype(BF16)
            a = _rec_mixer(h, gain, rec_w_in[j].astype(BF16), vecs, wxa, B=B, S=S, ts=tm)
            w_out = rec_w_out[j]
        h = _post_layer(a, h, p.reshape(depth, T, -1), layer, ln_mix_post, ln_mlp_pre, ln_mlp_post,
                        ple_norm, w_out, mlp_w_up, mlp_w_down, ple_w_proj, ple_w_gate, tm=tm)
    return h.reshape(B, S, D)
```

```python
import functools
import math

import jax
import jax.numpy as jnp
from jax import lax
from jax.experimental import pallas as pl
from jax.experimental.pallas import tpu as pltpu

F32 = jnp.float32
BF16 = jnp.bfloat16

LANES = 128
SUBLANES = 8
VMEM_LIMIT_BYTES = 56 * 1024 * 1024

HEAD_DIM = 64
UNIT = 2 * HEAD_DIM
NORM_EPS = 1e-6
SUBLN_EPS = 1e-5
RG_C = 8.0
CONV_WIDTH = 4
RNN_BLOCK_W = 128
MASKED = -1e30
LOG2_E = math.log2(math.e)


def _rms(x, gain, eps=NORM_EPS):
    return x * lax.rsqrt(jnp.mean(x * x, axis=-1, keepdims=True) + eps) * gain


def _dot(a, b):
    return jnp.dot(a, b, preferred_element_type=F32)


def _dot_nt(a, b):
    return lax.dot_general(a, b, (((1,), (1,)), ((), ())), preferred_element_type=F32)


def _log_sigmoid(x):
    return jnp.minimum(x, 0.0) - jnp.log1p(jnp.exp(-jnp.abs(x)))


def _resident(shape):
    zeros = (0,) * len(shape)
    return pl.BlockSpec(shape, lambda *_: zeros, pipeline_mode=pl.Buffered(1))


def _params(semantics):
    return pltpu.CompilerParams(dimension_semantics=semantics, vmem_limit_bytes=VMEM_LIMIT_BYTES)


def _attn_in_kernel(h_ref, g_ref, w_ref, wfz_ref, bf_ref, z_ref, lf_ref, *, n_chunk):
    hn = _rms(h_ref[...], g_ref[...]).astype(BF16)
    width = w_ref.shape[1] // n_chunk
    for c in range(n_chunk):
        cols = slice(c * width, (c + 1) * width)
        z_ref[:, cols] = _dot(hn, w_ref[:, cols]).astype(BF16)
    fz = _dot_nt(wfz_ref[...], hn)
    lf_ref[...] = _log_sigmoid(fz[:SUBLANES] + bf_ref[...])


def _attn_in(h, gain, w_main, w_fz_t, b_forget, *, tm):
    T, D = h.shape
    N = w_main.shape[1]
    return pl.pallas_call(
        functools.partial(_attn_in_kernel, n_chunk=N // 1024),
        grid=(T // tm,),
        in_specs=[
            pl.BlockSpec((tm, D), lambda i: (i, 0)),
            _resident((1, D)),
            _resident((D, N)),
            _resident(w_fz_t.shape),
            _resident(b_forget.shape),
        ],
        out_specs=[
            pl.BlockSpec((tm, N), lambda i: (i, 0)),
            pl.BlockSpec((SUBLANES, tm), lambda i: (0, i)),
        ],
        out_shape=[
            jax.ShapeDtypeStruct((T, N), BF16),
            jax.ShapeDtypeStruct((SUBLANES, T), F32),
        ],
        compiler_params=_params(("parallel",)),
        name="attn_in",
    )(h, gain, w_main, w_fz_t, b_forget)


def _fox_bias_kernel(lf_ref, cb_ref, *, n_diff_heads):
    x = lf_ref[...]
    S = x.shape[1]
    pos = lax.broadcasted_iota(jnp.int32, x.shape, 1)
    k = 1
    while k < S:
        x = x + jnp.where(pos >= k, pltpu.roll(x, k, axis=1), 0.0)
        k *= 2
    row = lax.broadcasted_iota(jnp.int32, x.shape, 0)
    head = (row // 2 + 1).astype(F32)
    slope = jnp.exp2(-8.0 * head / n_diff_heads)
    cb_ref[0:SUBLANES, :] = (LOG2_E * slope) * pos.astype(F32)
    cb_ref[SUBLANES:, :] = -LOG2_E * x


def _fox_bias(lf_t, *, B, S, n_diff_heads):
    return pl.pallas_call(
        functools.partial(_fox_bias_kernel, n_diff_heads=n_diff_heads),
        grid=(B,),
        in_specs=[pl.BlockSpec((SUBLANES, S), lambda b: (0, b))],
        out_specs=pl.BlockSpec((None, 2 * SUBLANES, S), lambda b: (b, 0, 0)),
        out_shape=jax.ShapeDtypeStruct((B, 2 * SUBLANES, S), F32),
        compiler_params=_params(("parallel",)),
        name="fox_bias",
    )(lf_t)


def _attention_kernel(q_ref, k_ref, v_ref, cb_ref, lp_ref, sg_ref, o_ref, qs_ref, m_ref, acc_ref, sc_a, sc_b,
                      *, tq, ups, n_diff_groups, lam_init):
    tk = tq // 2
    g = pl.program_id(1)
    qi = pl.program_id(2)
    lane = lax.broadcasted_iota(jnp.int32, (tq, UNIT), 1)
    zero = jnp.zeros((tq, UNIT), BF16)
    for j in range(ups):
        qu = q_ref[:, j * UNIT:(j + 1) * UNIT]
        qs_ref[2 * j] = jnp.where(lane < HEAD_DIM, qu, zero)
        qs_ref[2 * j + 1] = jnp.where(lane >= HEAD_DIM, qu, zero)
    m_ref[...] = jnp.full(m_ref.shape, MASKED, F32)
    acc_ref[...] = jnp.zeros(acc_ref.shape, F32)

    every, upper, lower = slice(0, tq), slice(0, tk), slice(tk, tq)

    def scores(ki, sc_ref, rows):
        start = pl.multiple_of(ki * tk, tk)
        for j in range(ups):
            k = k_ref[pl.ds(start, tk), j * UNIT:(j + 1) * UNIT]
            for s in (2 * j, 2 * j + 1):
                sc_ref[s, rows, :] = _dot_nt(qs_ref[s, rows, :], k) + cb_ref[s:s + 1, pl.ds(start, tk)]

    def softmax_pv(ki, sc_ref, rows, triangle):
        start = pl.multiple_of(ki * tk, tk)
        if triangle:
            r = lax.broadcasted_iota(jnp.int32, (tk, tk), 0)
            c = lax.broadcasted_iota(jnp.int32, (tk, tk), 1)
            visible = c <= r
        for j in range(ups):
            v = v_ref[pl.ds(start, tk), j * UNIT:(j + 1) * UNIT]
            v_ones = jnp.concatenate([v, jnp.ones_like(v)], axis=1)
            for s in (2 * j, 2 * j + 1):
                sc = sc_ref[s, rows, :]
                if triangle:
                    sc = jnp.where(visible, sc, MASKED)
                m_old = m_ref[s, rows, :]
                m_new = jnp.maximum(m_old, jnp.max(sc, axis=-1, keepdims=True))
                alpha = jnp.exp2(m_old - m_new)
                p = jnp.exp2(sc - jnp.tile(m_new, (1, tk // LANES)))
                acc_ref[s, rows, :] = (jnp.tile(alpha, (1, 2)) * acc_ref[s, rows, :]
                                       + _dot(p.astype(BF16), v_ones))
                m_ref[s, rows, :] = m_new

    scores(0, sc_a, every)

    def pair(t, carry):
        scores(2 * t + 1, sc_b, every)
        softmax_pv(2 * t, sc_a, every, triangle=False)
        scores(2 * t + 2, sc_a, every)
        softmax_pv(2 * t + 1, sc_b, every, triangle=False)
        return carry

    lax.fori_loop(0, qi, pair, 0)
    scores(2 * qi + 1, sc_b, lower)
    softmax_pv(2 * qi, sc_a, upper, triangle=True)
    softmax_pv(2 * qi, sc_a, lower, triangle=False)
    softmax_pv(2 * qi + 1, sc_b, lower, triangle=True)

    outs = [acc_ref[s, :, :UNIT] / acc_ref[s, :, UNIT:] for s in range(2 * ups)]

    @pl.when(g < n_diff_groups)
    def _():
        lp = lp_ref[...]
        lam = (jnp.exp(jnp.sum(lp[0:1] * lp[1:2], axis=-1, keepdims=True))
               - jnp.exp(jnp.sum(lp[2:3] * lp[3:4], axis=-1, keepdims=True)) + lam_init)
        for j in range(ups):
            d = outs[2 * j] - lam * outs[2 * j + 1]
            o_ref[:, j * UNIT:(j + 1) * UNIT] = (
                _rms(d, sg_ref[...], SUBLN_EPS) * (1.0 - lam_init)).astype(o_ref.dtype)

    @pl.when(g >= n_diff_groups)
    def _():
        for j in range(ups):
            o_ref[:, j * UNIT:(j + 1) * UNIT] = jnp.where(
                lane < HEAD_DIM, outs[2 * j], outs[2 * j + 1]).astype(o_ref.dtype)


def _attention(z, cb, lam_params, subln, *, B, S, tq, ups, n_diff_units, n_units, lam_init):
    T = B * S
    nq = S // tq
    n_groups = n_units // ups
    n_diff_groups = n_diff_units // ups
    n_fox_groups = n_groups - n_diff_groups
    def qcol(g):
        return jnp.where(g < n_diff_groups, g, g + 2 * n_diff_groups)

    def kcol(g):
        return jnp.where(g < n_diff_groups, g + n_diff_groups, g + 2 * n_diff_groups + n_fox_groups)

    def vcol(g):
        return jnp.where(g < n_diff_groups, g + 2 * n_diff_groups, g + 2 * n_diff_groups + 2 * n_fox_groups)

    cb4 = cb.reshape(B, n_groups, 2 * ups, S)
    return pl.pallas_call(
        functools.partial(_attention_kernel, tq=tq, ups=ups, n_diff_groups=n_diff_groups, lam_init=lam_init),
        grid=(B, n_groups, nq),
        in_specs=[
            pl.BlockSpec((tq, ups * UNIT), lambda b, g, i: (b * nq + i, qcol(g))),
            pl.BlockSpec((S, ups * UNIT), lambda b, g, i: (b, kcol(g))),
            pl.BlockSpec((S, ups * UNIT), lambda b, g, i: (b, vcol(g))),
            pl.BlockSpec((None, None, 2 * ups, S), lambda b, g, i: (b, g, 0, 0)),
            _resident(lam_params.shape),
            _resident(subln.shape),
        ],
        out_specs=pl.BlockSpec((tq, ups * UNIT), lambda b, g, i: (b * nq + i, g)),
        out_shape=jax.ShapeDtypeStruct((T, n_units * UNIT), BF16),
        scratch_shapes=[
            pltpu.VMEM((2 * ups, tq, UNIT), BF16),
            pltpu.VMEM((2 * ups, tq, LANES), F32),
            pltpu.VMEM((2 * ups, tq, 2 * UNIT), F32),
            pltpu.VMEM((2 * ups, tq, tq // 2), F32),
            pltpu.VMEM((2 * ups, tq, tq // 2), F32),
        ],
        compiler_params=_params(("parallel", "parallel", "arbitrary")),
        name="attention",
    )(z, z, z, cb4, lam_params, subln)


def _gelu_tanh(x):
    c = 2.0 * math.sqrt(2.0 / math.pi)
    return x * jax.nn.sigmoid(x * (c + (c * 0.044715) * (x * x)))


def _rec_kernel(h_ref, g_ref, win_ref, vec_ref, wxa_ref, o_ref, tail_ref, a_buf, b_buf, hstate, *, ts, width):
    si = pl.program_id(1)
    n_blocks = width // RNN_BLOCK_W
    n_groups = ts // SUBLANES

    @pl.when(si == 0)
    def _():
        tail_ref[...] = jnp.zeros(tail_ref.shape, F32)
        hstate[...] = jnp.zeros(hstate.shape, F32)

    hn = _rms(h_ref[...], g_ref[...]).astype(BF16)
    y = _gelu_tanh(_dot(hn, win_ref[:, :width]))
    xr = _dot(hn, win_ref[:, width:]).reshape(n_groups, SUBLANES, width)

    ext = jnp.concatenate([tail_ref[...][None], xr], axis=0)
    tail_ref[...] = xr[n_groups - 1]
    sub = lax.broadcasted_iota(jnp.int32, (1, SUBLANES, RNN_BLOCK_W), 1)
    sub_w = lax.broadcasted_iota(jnp.int32, (1, SUBLANES, width), 1)

    def vec(i):
        return vec_ref[i:i + 1, :].reshape(1, 1, width)

    xc = vec(CONV_WIDTH) + vec(CONV_WIDTH - 1) * xr
    for d in range(1, CONV_WIDTH):
        rot = pltpu.roll(ext, d, axis=1)
        xc = xc + vec(CONV_WIDTH - 1 - d) * jnp.where(sub_w >= d, rot[1:], rot[:-1])
    xc = xc.reshape(ts, width)

    log_sig_a = _log_sigmoid(vec_ref[7:8, :])
    start = jnp.logical_and(sub == 0, si == 0)
    for n in range(n_blocks):
        cols = slice(n * RNN_BLOCK_W, (n + 1) * RNN_BLOCK_W)
        xcn = xc[:, cols]
        g = _dot(xcn.astype(BF16), wxa_ref[n])
        gate_x = jax.nn.sigmoid(g[:, :RNN_BLOCK_W] + vec_ref[5:6, cols])
        gate_a = jax.nn.sigmoid(g[:, RNN_BLOCK_W:] + vec_ref[6:7, cols])
        a = jnp.exp(RG_C * gate_a * log_sig_a[:, cols]).reshape(n_groups, SUBLANES, RNN_BLOCK_W)
        a = jnp.concatenate([jnp.where(start, 0.0, a[:1]), a[1:]], axis=0)
        mult = jnp.sqrt(1.0 - a * a)
        b = mult * (gate_x * xcn).reshape(n_groups, SUBLANES, RNN_BLOCK_W)
        k = 1
        while k < SUBLANES:
            keep = sub >= k
            b = b + jnp.where(keep, a, 0.0) * pltpu.roll(b, k, axis=1)
            a = a * jnp.where(keep, pltpu.roll(a, k, axis=1), 1.0)
            k *= 2
        a_buf[:, :, cols] = a
        b_buf[:, :, cols] = b

    carry = hstate[...]
    for gidx in range(n_groups):
        rows = slice(gidx * SUBLANES, (gidx + 1) * SUBLANES)
        hg = a_buf[gidx] * carry + b_buf[gidx]
        o_ref[rows, :] = (hg * y[rows, :]).astype(o_ref.dtype)
        carry = hg[SUBLANES - 1:SUBLANES, :]
    hstate[...] = carry


def _rec_mixer(h, gain, w_in, vecs, wxa, *, B, S, ts):
    T, D = h.shape
    width = w_in.shape[1] // 2
    ns = S // ts
    return pl.pallas_call(
        functools.partial(_rec_kernel, ts=ts, width=width),
        grid=(B, ns),
        in_specs=[
            pl.BlockSpec((ts, D), lambda b, s: (b * ns + s, 0)),
            _resident((1, D)),
            _resident(w_in.shape),
            _resident(vecs.shape),
            _resident(wxa.shape),
        ],
        out_specs=pl.BlockSpec((ts, width), lambda b, s: (b * ns + s, 0)),
        out_shape=jax.ShapeDtypeStruct((T, width), BF16),
        scratch_shapes=[
            pltpu.VMEM((SUBLANES, width), F32),
            pltpu.VMEM((ts // SUBLANES, SUBLANES, width), F32),
            pltpu.VMEM((ts // SUBLANES, SUBLANES, width), F32),
            pltpu.VMEM((1, width), F32),
        ],
        compiler_params=_params(("arbitrary", "arbitrary")),
        name="rec_mixer",
    )(h, gain, w_in, vecs, wxa)


def _post_kernel(a_ref, h_ref, p_ref, g_ref, wout_ref, wup_ref, wdn_ref, wproj_ref, wgate_ref, o_ref,
                 *, ff_chunk):
    m = _dot(a_ref[...], wout_ref[...])
    h1 = h_ref[...] + _rms(m, g_ref[0:1, :])
    u = _rms(h1, g_ref[1:2, :]).astype(BF16)
    f = jnp.zeros_like(h1)
    for c in range(wup_ref.shape[1] // ff_chunk):
        cols = slice(c * ff_chunk, (c + 1) * ff_chunk)
        t = jnp.square(jnp.maximum(_dot(u, wup_ref[:, cols]), 0.0)).astype(BF16)
        f = f + _dot(t, wdn_ref[cols, :])
    h2 = h1 + _rms(f, g_ref[2:3, :])
    e = _rms(_dot(p_ref[...].astype(BF16), wproj_ref[...]), g_ref[3:4, :])
    gate = jax.nn.sigmoid(_dot(h2.astype(BF16), wgate_ref[...]))
    o_ref[...] = h2 + e * gate


def _post(a, h, p, gains, w_out, w_up, w_down, w_proj, w_gate, *, layer, tm):
    T, D = h.shape
    row = lambda i: (i, 0)
    return pl.pallas_call(
        functools.partial(_post_kernel, ff_chunk=1024),
        grid=(T // tm,),
        in_specs=[
            pl.BlockSpec((tm, a.shape[1]), row),
            pl.BlockSpec((tm, D), row),
            pl.BlockSpec((None, tm, p.shape[2]), lambda i: (layer, i, 0)),
            _resident(gains.shape),
            _resident(w_out.shape),
            _resident(w_up.shape),
            _resident(w_down.shape),
            _resident(w_proj.shape),
            _resident(w_gate.shape),
        ],
        out_specs=pl.BlockSpec((tm, D), row),
        out_shape=jax.ShapeDtypeStruct((T, D), F32),
        compiler_params=_params(("parallel",)),
        name="post",
    )(a, h, p, gains, w_out, w_up, w_down, w_proj, w_gate)


def _pad_rows(x, rows):
    return jnp.concatenate([x, jnp.zeros((rows - x.shape[0],) + x.shape[1:], x.dtype)], axis=0)


def _post_layer(a, h, p, layer, ln_mix_post, ln_mlp_pre, ln_mlp_post, ple_norm, w_out,
                mlp_w_up, mlp_w_down, ple_w_proj, ple_w_gate, *, tm):
    gains = _pad_rows(jnp.stack([ln_mix_post[layer], ln_mlp_pre[layer], ln_mlp_post[layer],
                                 ple_norm[layer]]), SUBLANES)
    return _post(a, h, p, gains, w_out.astype(BF16), mlp_w_up[layer].astype(BF16),
                 mlp_w_down[layer].astype(BF16), ple_w_proj[layer].astype(BF16),
                 ple_w_gate[layer].astype(BF16), layer=layer, tm=tm)


def kernel(x, p, ln_mix_pre, ln_mix_post, ln_mlp_pre, ln_mlp_post, mlp_w_up, mlp_w_down, ple_w_proj, ple_norm, ple_w_gate, attn_w_in, attn_b_forget, attn_w_out, diff_lambda_q1, diff_lambda_k1, diff_lambda_q2, diff_lambda_k2, diff_subln, rec_w_in, rec_conv_w, rec_conv_b, rec_wx, rec_bx, rec_wa, rec_ba, rec_a_param, rec_w_out):
    B, S, D = x.shape
    T = B * S
    depth = p.shape[0]
    tm = 512
    n_diff_heads = D // (4 * HEAD_DIM)
    n_fox_heads = D // (2 * HEAD_DIM)
    n_units = n_diff_heads + n_fox_heads // 2
    qkv_width = 3 * (n_diff_heads + n_fox_heads // 2) * UNIT
    assert S % (2 * tm) == 0 and attn_w_in.shape[2] == qkv_width + n_fox_heads == qkv_width + SUBLANES

    h = x.reshape(T, D)
    for layer in range(depth):
        j = layer // 2
        gain = ln_mix_pre[layer].reshape(1, D)
        if layer % 2 == 0:
            w_in = attn_w_in[j]
            col = jnp.arange(qkv_width)
            is_q = (col < n_diff_heads * UNIT) | ((col >= 3 * n_diff_heads * UNIT)
                                                  & (col < 3 * n_diff_heads * UNIT + n_fox_heads * HEAD_DIM))
            w_main = (w_in[:, :qkv_width] * jnp.where(is_q, LOG2_E * HEAD_DIM ** -0.5, 1.0)).astype(BF16)
            w_fz_t = _pad_rows(w_in[:, qkv_width:].T, 2 * SUBLANES).astype(BF16)
            z, lf_t = _attn_in(h, gain, w_main, w_fz_t, attn_b_forget[j].reshape(n_fox_heads, 1), tm=tm)
            cb = _fox_bias(lf_t, B=B, S=S, n_diff_heads=n_diff_heads)
            lam_params = _pad_rows(jnp.pad(
                jnp.stack([diff_lambda_q1[j], diff_lambda_k1[j], diff_lambda_q2[j], diff_lambda_k2[j]]),
                ((0, 0), (0, LANES - HEAD_DIM))), SUBLANES)
            lam_init = 0.8 - 0.6 * math.exp(-0.3 * layer)
            a = _attention(z, cb, lam_params, diff_subln[j].reshape(1, UNIT), B=B, S=S, tq=2 * tm, ups=2,
                           n_diff_units=n_diff_heads, n_units=n_units, lam_init=lam_init)
            w_out = attn_w_out[j]
        else:
            width = rec_w_in.shape[2] // 2
            vecs = jnp.concatenate([rec_conv_w[j], rec_conv_b[j][None], rec_bx[j][None],
                                    rec_ba[j][None], rec_a_param[j][None]], axis=0)
            wxa = jnp.concatenate([rec_wx[j], rec_wa[j]], axis=-1).astype(BF16)
            a = _rec_mixer(h, gain, rec_w_in[j].astype(BF16), vecs, wxa, B=B, S=S, ts=tm)
            w_out = rec_w_out[j]
        h = _post_layer(a, h, p.reshape(depth, T, -1), layer, ln_mix_post, ln_mlp_pre, ln_mlp_post,
                        ple_norm, w_out, mlp_w_up, mlp_w_down, ple_w_proj, ple_w_gate, tm=tm)
    return h.reshape(B, S, D)
```

```python
import functools
import math

import jax
import jax.numpy as jnp
from jax import lax
from jax.experimental import pallas as pl
from jax.experimental.pallas import tpu as pltpu

F32 = jnp.float32
BF16 = jnp.bfloat16

LANES = 128
SUBLANES = 8
VMEM_LIMIT_BYTES = 56 * 1024 * 1024

HEAD_DIM = 64
UNIT = 2 * HEAD_DIM
NORM_EPS = 1e-6
SUBLN_EPS = 1e-5
RG_C = 8.0
CONV_WIDTH = 4
RNN_BLOCK_W = 128
MASKED = -1e30
LOG2_E = math.log2(math.e)


def _rms(x, gain, eps=NORM_EPS):
    return x * lax.rsqrt(jnp.mean(x * x, axis=-1, keepdims=True) + eps) * gain


def _dot(a, b):
    return jnp.dot(a, b, preferred_element_type=F32)


def _dot_nt(a, b):
    return lax.dot_general(a, b, (((1,), (1,)), ((), ())), preferred_element_type=F32)


def _log_sigmoid(x):
    return jnp.minimum(x, 0.0) - jnp.log1p(jnp.exp(-jnp.abs(x)))


def _resident(shape):
    zeros = (0,) * len(shape)
    return pl.BlockSpec(shape, lambda *_: zeros, pipeline_mode=pl.Buffered(1))


def _params(semantics):
    return pltpu.CompilerParams(dimension_semantics=semantics, vmem_limit_bytes=VMEM_LIMIT_BYTES)


def _attn_in_kernel(h_ref, g_ref, w_ref, wfz_ref, bf_ref, z_ref, lf_ref, *, n_chunk):
    hn = _rms(h_ref[...], g_ref[...]).astype(BF16)
    width = w_ref.shape[1] // n_chunk
    for c in range(n_chunk):
        cols = slice(c * width, (c + 1) * width)
        z_ref[:, cols] = _dot(hn, w_ref[:, cols]).astype(BF16)
    fz = _dot_nt(wfz_ref[...], hn)
    lf_ref[...] = _log_sigmoid(fz[:SUBLANES] + bf_ref[...])


def _attn_in(h, gain, w_main, w_fz_t, b_forget, *, tm):
    T, D = h.shape
    N = w_main.shape[1]
    return pl.pallas_call(
        functools.partial(_attn_in_kernel, n_chunk=N // 1024),
        grid=(T // tm,),
        in_specs=[
            pl.BlockSpec((tm, D), lambda i: (i, 0)),
            _resident((1, D)),
            _resident((D, N)),
            _resident(w_fz_t.shape),
            _resident(b_forget.shape),
        ],
        out_specs=[
            pl.BlockSpec((tm, N), lambda i: (i, 0)),
            pl.BlockSpec((SUBLANES, tm), lambda i: (0, i)),
        ],
        out_shape=[
            jax.ShapeDtypeStruct((T, N), BF16),
            jax.ShapeDtypeStruct((SUBLANES, T), F32),
        ],
        compiler_params=_params(("parallel",)),
        name="attn_in",
    )(h, gain, w_main, w_fz_t, b_forget)


def _key_bias_kernel(lf_ref, sel_ref, f_ref, *, n_diff_heads, tt):
    x = lf_ref[...]
    S = x.shape[1]
    pos = lax.broadcasted_iota(jnp.int32, x.shape, 1)
    k = 1
    while k < S:
        x = x + jnp.where(pos >= k, pltpu.roll(x, k, axis=1), 0.0)
        k *= 2
    row = lax.broadcasted_iota(jnp.int32, x.shape, 0)
    head = (row // 2 + 1).astype(F32)
    slope = jnp.exp2(-8.0 * head / n_diff_heads)
    bias = jnp.concatenate([(LOG2_E * slope) * pos.astype(F32),
                            -LOG2_E * x], axis=0)
    hi = bias.astype(BF16)
    rest = bias - hi.astype(F32)
    mid = rest.astype(BF16)
    lo = (rest - mid.astype(F32)).astype(BF16)
    terms = jnp.concatenate([hi, mid, lo], axis=0)
    eye = jnp.where(lax.broadcasted_iota(jnp.int32, (tt, tt), 0) == lax.broadcasted_iota(jnp.int32, (tt, tt), 1),
                    1.0, 0.0).astype(BF16)
    for t in range(S // tt):
        by_key = _dot_nt(eye, terms[:, t * tt:(t + 1) * tt]).astype(BF16)
        f_ref[t * tt:(t + 1) * tt, :] = _dot(by_key, sel_ref[...]).astype(BF16)


def _key_bias_selection(n_units):
    rows = jnp.arange(n_units * UNIT)
    unit, lane = rows // UNIT, rows % UNIT
    part = lane % HEAD_DIM
    stream = 2 * unit + jnp.where(lane >= HEAD_DIM, 0, 1)
    col = part * (2 * n_units) + stream
    hot = (jnp.arange(3 * 2 * n_units)[:, None] == col[None, :]) & (part < 3)[None, :]
    return hot.astype(BF16)


def _key_bias(lf_t, *, B, S, n_diff_heads, n_units):
    sel = _key_bias_selection(n_units)
    return pl.pallas_call(
        functools.partial(_key_bias_kernel, n_diff_heads=n_diff_heads, tt=512),
        grid=(B,),
        in_specs=[pl.BlockSpec((SUBLANES, S), lambda b: (0, b)), _resident(sel.shape)],
        out_specs=pl.BlockSpec((S, n_units * UNIT), lambda b: (b, 0)),
        out_shape=jax.ShapeDtypeStruct((B * S, n_units * UNIT), BF16),
        compiler_params=_params(("parallel",)),
        name="key_bias",
    )(lf_t, sel)


def _attention_kernel(q_ref, k_ref, v_ref, f_ref, lp_ref, sg_ref, o_ref, qs_ref, m_ref, acc_ref, sc_a, sc_b,
                      *, tq, ups, n_diff_groups, lam_init):
    tk = tq // 2
    g = pl.program_id(1)
    qi = pl.program_id(2)
    lane = lax.broadcasted_iota(jnp.int32, (tq, UNIT), 1)
    lane_k = lax.broadcasted_iota(jnp.int32, (tk, UNIT), 1)
    part = lane % HEAD_DIM
    ones_hi = jnp.where((lane >= HEAD_DIM) & (part < 3), 1.0, 0.0).astype(BF16)
    ones_lo = jnp.where((lane < HEAD_DIM) & (part < 3), 1.0, 0.0).astype(BF16)
    for j in range(ups):
        qu = q_ref[:, j * UNIT:(j + 1) * UNIT]
        qs_ref[2 * j] = jnp.where(lane < HEAD_DIM, qu, ones_hi)
        qs_ref[2 * j + 1] = jnp.where(lane >= HEAD_DIM, qu, ones_lo)
    m_ref[...] = jnp.full(m_ref.shape, MASKED, F32)
    acc_ref[...] = jnp.zeros(acc_ref.shape, F32)

    every, upper, lower = slice(0, tq), slice(0, tk), slice(tk, tq)

    def scores(ki, sc_ref, rows, s):
        start = pl.multiple_of(ki * tk, tk)
        cols = slice((s // 2) * UNIT, (s // 2 + 1) * UNIT)
        k = k_ref[pl.ds(start, tk), cols]
        f = f_ref[pl.ds(start, tk), cols]
        own = lane_k < HEAD_DIM if s % 2 == 0 else lane_k >= HEAD_DIM
        sc_ref[s, rows, :] = _dot_nt(qs_ref[s, rows, :], jnp.where(own, k, f))

    def softmax_pv(ki, sc_ref, rows, s, triangle):
        start = pl.multiple_of(ki * tk, tk)
        v = v_ref[pl.ds(start, tk), (s // 2) * UNIT:(s // 2 + 1) * UNIT]
        v_ones = jnp.concatenate([v, jnp.ones_like(v)], axis=1)
        sc = sc_ref[s, rows, :]
        if triangle:
            r = lax.broadcasted_iota(jnp.int32, (tk, tk), 0)
            c = lax.broadcasted_iota(jnp.int32, (tk, tk), 1)
            sc = jnp.where(c <= r, sc, MASKED)
        m_old = m_ref[s, rows, :]
        m_new = jnp.maximum(m_old, jnp.max(sc, axis=-1, keepdims=True))
        alpha = jnp.exp2(m_old - m_new)
        p = jnp.exp2(sc - jnp.tile(m_new, (1, tk // LANES)))
        acc_ref[s, rows, :] = jnp.tile(alpha, (1, 2)) * acc_ref[s, rows, :] + _dot(p.astype(BF16), v_ones)
        m_ref[s, rows, :] = m_new

    streams = range(2 * ups)
    for s in streams:
        scores(0, sc_a, every, s)

    def pair(t, carry):
        for s in streams:
            scores(2 * t + 1, sc_b, every, s)
            softmax_pv(2 * t, sc_a, every, s, triangle=False)
        for s in streams:
            scores(2 * t + 2, sc_a, every, s)
            softmax_pv(2 * t + 1, sc_b, every, s, triangle=False)
        return carry

    lax.fori_loop(0, qi, pair, 0)
    for s in streams:
        scores(2 * qi + 1, sc_b, lower, s)
        softmax_pv(2 * qi, sc_a, upper, s, triangle=True)
        softmax_pv(2 * qi, sc_a, lower, s, triangle=False)
    for s in streams:
        softmax_pv(2 * qi + 1, sc_b, lower, s, triangle=True)

    outs = [acc_ref[s, :, :UNIT] / acc_ref[s, :, UNIT:] for s in range(2 * ups)]

    @pl.when(g < n_diff_groups)
    def _():
        lp = lp_ref[...]
        lam = (jnp.exp(jnp.sum(lp[0:1] * lp[1:2], axis=-1, keepdims=True))
               - jnp.exp(jnp.sum(lp[2:3] * lp[3:4], axis=-1, keepdims=True)) + lam_init)
        for j in range(ups):
            d = outs[2 * j] - lam * outs[2 * j + 1]
            o_ref[:, j * UNIT:(j + 1) * UNIT] = (
                _rms(d, sg_ref[...], SUBLN_EPS) * (1.0 - lam_init)).astype(o_ref.dtype)

    @pl.when(g >= n_diff_groups)
    def _():
        for j in range(ups):
            o_ref[:, j * UNIT:(j + 1) * UNIT] = jnp.where(
                lane < HEAD_DIM, outs[2 * j], outs[2 * j + 1]).astype(o_ref.dtype)


def _attention(z, key_bias, lam_params, subln, *, B, S, tq, ups, n_diff_units, n_units, lam_init):
    T = B * S
    nq = S // tq
    n_groups = n_units // ups
    n_diff_groups = n_diff_units // ups
    n_fox_groups = n_groups - n_diff_groups
    def qcol(g):
        return jnp.where(g < n_diff_groups, g, g + 2 * n_diff_groups)

    def kcol(g):
        return jnp.where(g < n_diff_groups, g + n_diff_groups, g + 2 * n_diff_groups + n_fox_groups)

    def vcol(g):
        return jnp.where(g < n_diff_groups, g + 2 * n_diff_groups, g + 2 * n_diff_groups + 2 * n_fox_groups)

    return pl.pallas_call(
        functools.partial(_attention_kernel, tq=tq, ups=ups, n_diff_groups=n_diff_groups, lam_init=lam_init),
        grid=(B, n_groups, nq),
        in_specs=[
            pl.BlockSpec((tq, ups * UNIT), lambda b, g, i: (b * nq + i, qcol(g))),
            pl.BlockSpec((S, ups * UNIT), lambda b, g, i: (b, kcol(g))),
            pl.BlockSpec((S, ups * UNIT), lambda b, g, i: (b, vcol(g))),
            pl.BlockSpec((S, ups * UNIT), lambda b, g, i: (b, g)),
            _resident(lam_params.shape),
            _resident(subln.shape),
        ],
        out_specs=pl.BlockSpec((tq, ups * UNIT), lambda b, g, i: (b * nq + i, g)),
        out_shape=jax.ShapeDtypeStruct((T, n_units * UNIT), BF16),
        scratch_shapes=[
            pltpu.VMEM((2 * ups, tq, UNIT), BF16),
            pltpu.VMEM((2 * ups, tq, LANES), F32),
            pltpu.VMEM((2 * ups, tq, 2 * UNIT), F32),
            pltpu.VMEM((2 * ups, tq, tq // 2), F32),
            pltpu.VMEM((2 * ups, tq, tq // 2), F32),
        ],
        compiler_params=_params(("parallel", "parallel", "arbitrary")),
        name="attention",
    )(z, z, z, key_bias, lam_params, subln)


def _sigmoid_exp2(z):
    return 1.0 / (1.0 + jnp.exp2(z))


def _gelu_tanh(x):
    c = -LOG2_E * 2.0 * math.sqrt(2.0 / math.pi)
    return x * _sigmoid_exp2(x * (c + (c * 0.044715) * (x * x)))


def _rec_kernel(h_ref, g_ref, win_ref, vec_ref, wxa_ref, o_ref, tail_ref, a_buf, b_buf, hstate, *, ts, width):
    si = pl.program_id(1)
    n_blocks = width // RNN_BLOCK_W
    n_groups = ts // SUBLANES

    @pl.when(si == 0)
    def _():
        tail_ref[...] = jnp.zeros(tail_ref.shape, F32)
        hstate[...] = jnp.zeros(hstate.shape, F32)

    hn = _rms(h_ref[...], g_ref[...]).astype(BF16)
    y = _gelu_tanh(_dot(hn, win_ref[:, :width]))
    xr = _dot(hn, win_ref[:, width:]).reshape(n_groups, SUBLANES, width)

    sub = lax.broadcasted_iota(jnp.int32, (1, SUBLANES, RNN_BLOCK_W), 1)
    not_first_w = lax.broadcasted_iota(jnp.int32, (1, SUBLANES, width), 1) >= 1

    def vec(i):
        return vec_ref[i:i + 1, :].reshape(1, 1, width)

    xc = vec(0) * xr
    for w in range(1, CONV_WIDTH):
        ext = jnp.concatenate([tail_ref[w - 1][None], xc], axis=0)
        tail_ref[w - 1] = xc[n_groups - 1]
        rot = pltpu.roll(ext, 1, axis=1)
        xc = vec(w) * xr + jnp.where(not_first_w, rot[1:], rot[:-1])
    xc = (xc + vec(CONV_WIDTH)).reshape(ts, width)

    bias_x = -LOG2_E * vec_ref[5:6, :]
    bias_a = -LOG2_E * vec_ref[6:7, :]
    log2_a_unit = (RG_C * LOG2_E) * _log_sigmoid(vec_ref[7:8, :])
    start = jnp.logical_and(sub == 0, si == 0)
    for n in range(n_blocks):
        cols = slice(n * RNN_BLOCK_W, (n + 1) * RNN_BLOCK_W)
        xcn = xc[:, cols]
        g = _dot(xcn.astype(BF16), wxa_ref[n])
        gate_x = _sigmoid_exp2(g[:, :RNN_BLOCK_W] + bias_x[:, cols])
        gate_a = _sigmoid_exp2(g[:, RNN_BLOCK_W:] + bias_a[:, cols])
        a = jnp.exp2(gate_a * log2_a_unit[:, cols]).reshape(n_groups, SUBLANES, RNN_BLOCK_W)
        a = jnp.concatenate([jnp.where(start, 0.0, a[:1]), a[1:]], axis=0)
        mult = jnp.exp2(0.5 * jnp.log2(1.0 - a * a))
        b = mult * (gate_x * xcn).reshape(n_groups, SUBLANES, RNN_BLOCK_W)
        k = 1
        while k < SUBLANES:
            keep = sub >= k
            b = b + jnp.where(keep, a, 0.0) * pltpu.roll(b, k, axis=1)
            a = a * jnp.where(keep, pltpu.roll(a, k, axis=1), 1.0)
            k *= 2
        a_buf[:, :, cols] = a
        b_buf[:, :, cols] = b

    carry = hstate[...]
    for gidx in range(n_groups):
        rows = slice(gidx * SUBLANES, (gidx + 1) * SUBLANES)
        hg = a_buf[gidx] * carry + b_buf[gidx]
        o_ref[rows, :] = (hg * y[rows, :]).astype(o_ref.dtype)
        carry = hg[SUBLANES - 1:SUBLANES, :]
    hstate[...] = carry


def _rec_mixer(h, gain, w_in, vecs, wxa, *, B, S, ts):
    T, D = h.shape
    width = w_in.shape[1] // 2
    ns = S // ts
    return pl.pallas_call(
        functools.partial(_rec_kernel, ts=ts, width=width),
        grid=(B, ns),
        in_specs=[
            pl.BlockSpec((ts, D), lambda b, s: (b * ns + s, 0)),
            _resident((1, D)),
            _resident(w_in.shape),
            _resident(vecs.shape),
            _resident(wxa.shape),
        ],
        out_specs=pl.BlockSpec((ts, width), lambda b, s: (b * ns + s, 0)),
        out_shape=jax.ShapeDtypeStruct((T, width), BF16),
        scratch_shapes=[
            pltpu.VMEM((CONV_WIDTH - 1, SUBLANES, width), F32),
            pltpu.VMEM((ts // SUBLANES, SUBLANES, width), F32),
            pltpu.VMEM((ts // SUBLANES, SUBLANES, width), F32),
            pltpu.VMEM((1, width), F32),
        ],
        compiler_params=_params(("arbitrary", "arbitrary")),
        name="rec_mixer",
    )(h, gain, w_in, vecs, wxa)


def _post_kernel(a_ref, h_ref, p_ref, g_ref, wout_ref, wup_ref, wdn_ref, wproj_ref, wgate_ref, o_ref,
                 *, ff_chunk):
    m = _dot(a_ref[...], wout_ref[...])
    h1 = h_ref[...] + _rms(m, g_ref[0:1, :])
    u = _rms(h1, g_ref[1:2, :]).astype(BF16)
    f = jnp.zeros_like(h1)
    for c in range(wup_ref.shape[1] // ff_chunk):
        cols = slice(c * ff_chunk, (c + 1) * ff_chunk)
        t = jnp.square(jnp.maximum(_dot(u, wup_ref[:, cols]), 0.0)).astype(BF16)
        f = f + _dot(t, wdn_ref[cols, :])
    h2 = h1 + _rms(f, g_ref[2:3, :])
    e = _rms(_dot(p_ref[...].astype(BF16), wproj_ref[...]), g_ref[3:4, :])
    gate = jax.nn.sigmoid(_dot(h2.astype(BF16), wgate_ref[...]))
    o_ref[...] = h2 + e * gate


def _post(a, h, p, gains, w_out, w_up, w_down, w_proj, w_gate, *, layer, tm):
    T, D = h.shape
    row = lambda i: (i, 0)
    return pl.pallas_call(
        functools.partial(_post_kernel, ff_chunk=1024),
        grid=(T // tm,),
        in_specs=[
            pl.BlockSpec((tm, a.shape[1]), row),
            pl.BlockSpec((tm, D), row),
            pl.BlockSpec((None, tm, p.shape[2]), lambda i: (layer, i, 0)),
            _resident(gains.shape),
            _resident(w_out.shape),
            _resident(w_up.shape),
            _resident(w_down.shape),
            _resident(w_proj.shape),
            _resident(w_gate.shape),
        ],
        out_specs=pl.BlockSpec((tm, D), row),
        out_shape=jax.ShapeDtypeStruct((T, D), F32),
        compiler_params=_params(("parallel",)),
        name="post",
    )(a, h, p, gains, w_out, w_up, w_down, w_proj, w_gate)


def _pad_rows(x, rows):
    return jnp.concatenate([x, jnp.zeros((rows - x.shape[0],) + x.shape[1:], x.dtype)], axis=0)


def _post_layer(a, h, p, layer, ln_mix_post, ln_mlp_pre, ln_mlp_post, ple_norm, w_out,
                mlp_w_up, mlp_w_down, ple_w_proj, ple_w_gate, *, tm):
    gains = _pad_rows(jnp.stack([ln_mix_post[layer], ln_mlp_pre[layer], ln_mlp_post[layer],
                                 ple_norm[layer]]), SUBLANES)
    return _post(a, h, p, gains, w_out.astype(BF16), mlp_w_up[layer].astype(BF16),
                 mlp_w_down[layer].astype(BF16), ple_w_proj[layer].astype(BF16),
                 ple_w_gate[layer].astype(BF16), layer=layer, tm=tm)


def kernel(x, p, ln_mix_pre, ln_mix_post, ln_mlp_pre, ln_mlp_post, mlp_w_up, mlp_w_down, ple_w_proj, ple_norm, ple_w_gate, attn_w_in, attn_b_forget, attn_w_out, diff_lambda_q1, diff_lambda_k1, diff_lambda_q2, diff_lambda_k2, diff_subln, rec_w_in, rec_conv_w, rec_conv_b, rec_wx, rec_bx, rec_wa, rec_ba, rec_a_param, rec_w_out):
    B, S, D = x.shape
    T = B * S
    depth = p.shape[0]
    tm = 512
    n_diff_heads = D // (4 * HEAD_DIM)
    n_fox_heads = D // (2 * HEAD_DIM)
    n_units = n_diff_heads + n_fox_heads // 2
    qkv_width = 3 * (n_diff_heads + n_fox_heads // 2) * UNIT
    assert S % (2 * tm) == 0 and attn_w_in.shape[2] == qkv_width + n_fox_heads == qkv_width + SUBLANES

    h = x.reshape(T, D)
    for layer in range(depth):
        j = layer // 2
        gain = ln_mix_pre[layer].reshape(1, D)
        if layer % 2 == 0:
            w_in = attn_w_in[j]
            col = jnp.arange(qkv_width)
            is_q = (col < n_diff_heads * UNIT) | ((col >= 3 * n_diff_heads * UNIT)
                                                  & (col < 3 * n_diff_heads * UNIT + n_fox_heads * HEAD_DIM))
            w_main = (w_in[:, :qkv_width] * jnp.where(is_q, LOG2_E * HEAD_DIM ** -0.5, 1.0)).astype(BF16)
            w_fz_t = _pad_rows(w_in[:, qkv_width:].T, 2 * SUBLANES).astype(BF16)
            z, lf_t = _attn_in(h, gain, w_main, w_fz_t, attn_b_forget[j].reshape(n_fox_heads, 1), tm=tm)
            key_bias = _key_bias(lf_t, B=B, S=S, n_diff_heads=n_diff_heads, n_units=n_units)
            lam_params = _pad_rows(jnp.pad(
                jnp.stack([diff_lambda_q1[j], diff_lambda_k1[j], diff_lambda_q2[j], diff_lambda_k2[j]]),
                ((0, 0), (0, LANES - HEAD_DIM))), SUBLANES)
            lam_init = 0.8 - 0.6 * math.exp(-0.3 * layer)
            a = _attention(z, key_bias, lam_params, diff_subln[j].reshape(1, UNIT), B=B, S=S, tq=2 * tm, ups=2,
                           n_diff_units=n_diff_heads, n_units=n_units, lam_init=lam_init)
            w_out = attn_w_out[j]
        else:
            width = rec_w_in.shape[2] // 2
            vecs = jnp.concatenate([rec_conv_w[j], rec_conv_b[j][None], rec_bx[j][None],
                                    rec_ba[j][None], rec_a_param[j][None]], axis=0)
            wxa = (-LOG2_E * jnp.concatenate([rec_wx[j], rec_wa[j]], axis=-1)).astype(BF16)
            a = _rec_mixer(h, gain, rec_w_in[j].astype(BF16), vecs, wxa, B=B, S=S, ts=tm)
            w_out = rec_w_out[j]
        h = _post_layer(a, h, p.reshape(depth, T, -1), layer, ln_mix_post, ln_mlp_pre, ln_mlp_post,
                        ple_norm, w_out, mlp_w_up, mlp_w_down, ple_w_proj, ple_w_gate, tm=tm)
    return h.reshape(B, S, D)
```

```python
import functools
import math

import jax
import jax.numpy as jnp
from jax import lax
from jax.experimental import pallas as pl
from jax.experimental.pallas import tpu as pltpu

F32 = jnp.float32
BF16 = jnp.bfloat16

LANES = 128
SUBLANES = 8
VMEM_LIMIT_BYTES = 56 * 1024 * 1024

HEAD_DIM = 64
UNIT = 2 * HEAD_DIM
NORM_EPS = 1e-6
SUBLN_EPS = 1e-5
RG_C = 8.0
CONV_WIDTH = 4
RNN_BLOCK_W = 128
MASKED = -1e30
LOG2_E = math.log2(math.e)


def _rms(x, gain, eps=NORM_EPS):
    return x * lax.rsqrt(jnp.mean(x * x, axis=-1, keepdims=True) + eps) * gain


def _dot(a, b):
    return jnp.dot(a, b, preferred_element_type=F32)


def _dot_nt(a, b):
    return lax.dot_general(a, b, (((1,), (1,)), ((), ())), preferred_element_type=F32)


def _log_sigmoid(x):
    return jnp.minimum(x, 0.0) - jnp.log1p(jnp.exp(-jnp.abs(x)))


def _resident(shape):
    zeros = (0,) * len(shape)
    return pl.BlockSpec(shape, lambda *_: zeros, pipeline_mode=pl.Buffered(1))


def _params(semantics):
    return pltpu.CompilerParams(dimension_semantics=semantics, vmem_limit_bytes=VMEM_LIMIT_BYTES)


def _attn_in_kernel(h_ref, g_ref, w_ref, wfz_ref, bf_ref, z_ref, lf_ref, *, n_chunk):
    hn = _rms(h_ref[...], g_ref[...]).astype(BF16)
    width = w_ref.shape[1] // n_chunk
    for c in range(n_chunk):
        cols = slice(c * width, (c + 1) * width)
        z_ref[:, cols] = _dot(hn, w_ref[:, cols]).astype(BF16)
    fz = _dot_nt(wfz_ref[...], hn)
    lf_ref[...] = _log_sigmoid(fz[:SUBLANES] + bf_ref[...])


def _attn_in(h, gain, w_main, w_fz_t, b_forget, *, tm):
    T, D = h.shape
    N = w_main.shape[1]
    return pl.pallas_call(
        functools.partial(_attn_in_kernel, n_chunk=N // 1024),
        grid=(T // tm,),
        in_specs=[
            pl.BlockSpec((tm, D), lambda i: (i, 0)),
            _resident((1, D)),
            _resident((D, N)),
            _resident(w_fz_t.shape),
            _resident(b_forget.shape),
        ],
        out_specs=[
            pl.BlockSpec((tm, N), lambda i: (i, 0)),
            pl.BlockSpec((SUBLANES, tm), lambda i: (0, i)),
        ],
        out_shape=[
            jax.ShapeDtypeStruct((T, N), BF16),
            jax.ShapeDtypeStruct((SUBLANES, T), F32),
        ],
        compiler_params=_params(("parallel",)),
        name="attn_in",
    )(h, gain, w_main, w_fz_t, b_forget)


def _key_bias_kernel(lf_ref, sel_ref, f_ref, *, n_diff_heads, tt):
    x = lf_ref[...]
    S = x.shape[1]
    pos = lax.broadcasted_iota(jnp.int32, x.shape, 1)
    k = 1
    while k < S:
        x = x + jnp.where(pos >= k, pltpu.roll(x, k, axis=1), 0.0)
        k *= 2
    row = lax.broadcasted_iota(jnp.int32, x.shape, 0)
    head = (row // 2 + 1).astype(F32)
    slope = jnp.exp2(-8.0 * head / n_diff_heads)
    bias = jnp.concatenate([(LOG2_E * slope) * pos.astype(F32),
                            -LOG2_E * x], axis=0)
    hi = bias.astype(BF16)
    rest = bias - hi.astype(F32)
    mid = rest.astype(BF16)
    lo = (rest - mid.astype(F32)).astype(BF16)
    terms = jnp.concatenate([hi, mid, lo], axis=0)
    eye = jnp.where(lax.broadcasted_iota(jnp.int32, (tt, tt), 0) == lax.broadcasted_iota(jnp.int32, (tt, tt), 1),
                    1.0, 0.0).astype(BF16)
    for t in range(S // tt):
        by_key = _dot_nt(eye, terms[:, t * tt:(t + 1) * tt]).astype(BF16)
        f_ref[t * tt:(t + 1) * tt, :] = _dot(by_key, sel_ref[...]).astype(BF16)


def _key_bias_selection(n_units):
    rows = jnp.arange(n_units * UNIT)
    unit, lane = rows // UNIT, rows % UNIT
    part = lane % HEAD_DIM
    stream = 2 * unit + jnp.where(lane >= HEAD_DIM, 0, 1)
    col = part * (2 * n_units) + stream
    hot = (jnp.arange(3 * 2 * n_units)[:, None] == col[None, :]) & (part < 3)[None, :]
    return hot.astype(BF16)


def _key_bias(lf_t, *, B, S, n_diff_heads, n_units):
    sel = _key_bias_selection(n_units)
    return pl.pallas_call(
        functools.partial(_key_bias_kernel, n_diff_heads=n_diff_heads, tt=512),
        grid=(B,),
        in_specs=[pl.BlockSpec((SUBLANES, S), lambda b: (0, b)), _resident(sel.shape)],
        out_specs=pl.BlockSpec((S, n_units * UNIT), lambda b: (b, 0)),
        out_shape=jax.ShapeDtypeStruct((B * S, n_units * UNIT), BF16),
        compiler_params=_params(("parallel",)),
        name="key_bias",
    )(lf_t, sel)


def _attention_kernel(q_ref, k_ref, v_ref, f_ref, lp_ref, sg_ref, o_ref, qs_ref, m_ref, acc_ref, sc_a, sc_b,
                      *, tq, ups, n_diff_groups, lam_init):
    tk = tq // 2
    g = pl.program_id(1)
    qi = pl.program_id(2)
    lane = lax.broadcasted_iota(jnp.int32, (tq, UNIT), 1)
    lane_k = lax.broadcasted_iota(jnp.int32, (tk, UNIT), 1)
    part = lane % HEAD_DIM
    ones_hi = jnp.where((lane >= HEAD_DIM) & (part < 3), 1.0, 0.0).astype(BF16)
    ones_lo = jnp.where((lane < HEAD_DIM) & (part < 3), 1.0, 0.0).astype(BF16)
    for j in range(ups):
        qu = q_ref[:, j * UNIT:(j + 1) * UNIT]
        qs_ref[2 * j] = jnp.where(lane < HEAD_DIM, qu, ones_hi)
        qs_ref[2 * j + 1] = jnp.where(lane >= HEAD_DIM, qu, ones_lo)

    every, upper, lower = slice(0, tq), slice(0, tk), slice(tk, tq)

    def scores(ki, sc_ref, rows, s):
        start = pl.multiple_of(ki * tk, tk)
        cols = slice((s // 2) * UNIT, (s // 2 + 1) * UNIT)
        k = k_ref[pl.ds(start, tk), cols]
        f = f_ref[pl.ds(start, tk), cols]
        own = lane_k < HEAD_DIM if s % 2 == 0 else lane_k >= HEAD_DIM
        sc_ref[s, rows, :] = _dot_nt(qs_ref[s, rows, :], jnp.where(own, k, f))

    def softmax_pv(ki, sc_ref, rows, s, triangle):
        start = pl.multiple_of(ki * tk, tk)
        v = v_ref[pl.ds(start, tk), (s // 2) * UNIT:(s // 2 + 1) * UNIT]
        v_ones = jnp.concatenate([v, jnp.ones_like(v)], axis=1)
        sc = sc_ref[s, rows, :]
        if triangle:
            r = lax.broadcasted_iota(jnp.int32, (tk, tk), 0)
            c = lax.broadcasted_iota(jnp.int32, (tk, tk), 1)
            sc = jnp.where(c <= r, sc, MASKED)
        m_old = m_ref[s, rows, :]
        m_new = jnp.maximum(m_old, jnp.max(sc, axis=-1, keepdims=True))
        alpha = jnp.exp2(m_old - m_new)
        p = jnp.exp2(sc - jnp.tile(m_new, (1, tk // LANES)))
        acc_ref[s, rows, :] = jnp.tile(alpha, (1, 2)) * acc_ref[s, rows, :] + _dot(p.astype(BF16), v_ones)
        m_ref[s, rows, :] = m_new

    def reset_state():
        m_ref[...] = jnp.full(m_ref.shape, MASKED, F32)
        acc_ref[...] = jnp.zeros(acc_ref.shape, F32)

    @pl.when((pl.program_id(0) == 0) & (g == 0) & (qi == 0))
    def _():
        reset_state()

    streams = range(2 * ups)
    for s in streams:
        scores(0, sc_a, every, s)

    def pair(t, carry):
        for s in streams:
            scores(2 * t + 1, sc_b, every, s)
            softmax_pv(2 * t, sc_a, every, s, triangle=False)
        for s in streams:
            scores(2 * t + 2, sc_a, every, s)
            softmax_pv(2 * t + 1, sc_b, every, s, triangle=False)
        return carry

    lax.fori_loop(0, qi, pair, 0)
    for s in streams:
        scores(2 * qi + 1, sc_b, lower, s)
        softmax_pv(2 * qi, sc_a, upper, s, triangle=True)
        softmax_pv(2 * qi, sc_a, lower, s, triangle=False)
    for s in streams:
        softmax_pv(2 * qi + 1, sc_b, lower, s, triangle=True)

    outs = [acc_ref[s, :, :UNIT] / acc_ref[s, :, UNIT:] for s in range(2 * ups)]
    reset_state()

    @pl.when(g < n_diff_groups)
    def _():
        lp = lp_ref[...]
        lam = (jnp.exp(jnp.sum(lp[0:1] * lp[1:2], axis=-1, keepdims=True))
               - jnp.exp(jnp.sum(lp[2:3] * lp[3:4], axis=-1, keepdims=True)) + lam_init)
        for j in range(ups):
            d = outs[2 * j] - lam * outs[2 * j + 1]
            o_ref[:, j * UNIT:(j + 1) * UNIT] = (
                _rms(d, sg_ref[...], SUBLN_EPS) * (1.0 - lam_init)).astype(o_ref.dtype)

    @pl.when(g >= n_diff_groups)
    def _():
        for j in range(ups):
            o_ref[:, j * UNIT:(j + 1) * UNIT] = jnp.where(
                lane < HEAD_DIM, outs[2 * j], outs[2 * j + 1]).astype(o_ref.dtype)


def _attention(z, key_bias, lam_params, subln, *, B, S, tq, ups, n_diff_units, n_units, lam_init):
    T = B * S
    nq = S // tq
    n_groups = n_units // ups
    n_diff_groups = n_diff_units // ups
    n_fox_groups = n_groups - n_diff_groups
    def qcol(g):
        return jnp.where(g < n_diff_groups, g, g + 2 * n_diff_groups)

    def kcol(g):
        return jnp.where(g < n_diff_groups, g + n_diff_groups, g + 2 * n_diff_groups + n_fox_groups)

    def vcol(g):
        return jnp.where(g < n_diff_groups, g + 2 * n_diff_groups, g + 2 * n_diff_groups + 2 * n_fox_groups)

    return pl.pallas_call(
        functools.partial(_attention_kernel, tq=tq, ups=ups, n_diff_groups=n_diff_groups, lam_init=lam_init),
        grid=(B, n_groups, nq),
        in_specs=[
            pl.BlockSpec((tq, ups * UNIT), lambda b, g, i: (b * nq + i, qcol(g))),
            pl.BlockSpec((S, ups * UNIT), lambda b, g, i: (b, kcol(g))),
            pl.BlockSpec((S, ups * UNIT), lambda b, g, i: (b, vcol(g))),
            pl.BlockSpec((S, ups * UNIT), lambda b, g, i: (b, g)),
            _resident(lam_params.shape),
            _resident(subln.shape),
        ],
        out_specs=pl.BlockSpec((tq, ups * UNIT), lambda b, g, i: (b * nq + i, g)),
        out_shape=jax.ShapeDtypeStruct((T, n_units * UNIT), BF16),
        scratch_shapes=[
            pltpu.VMEM((2 * ups, tq, UNIT), BF16),
            pltpu.VMEM((2 * ups, tq, LANES), F32),
            pltpu.VMEM((2 * ups, tq, 2 * UNIT), F32),
            pltpu.VMEM((2 * ups, tq, tq // 2), F32),
            pltpu.VMEM((2 * ups, tq, tq // 2), F32),
        ],
        compiler_params=_params(("arbitrary", "arbitrary", "arbitrary")),
        name="attention",
    )(z, z, z, key_bias, lam_params, subln)


def _sigmoid_exp2(z):
    return 1.0 / (1.0 + jnp.exp2(z))


def _gelu_tanh(x):
    c = -LOG2_E * 2.0 * math.sqrt(2.0 / math.pi)
    return x * _sigmoid_exp2(x * (c + (c * 0.044715) * (x * x)))


def _run(stages):
    for _ in stages:
        pass


def _rec_stages(h, g_ref, win_ref, vec_ref, wxa_ref, o_ref, tail_ref, a_buf, b_buf, hstate, *, is_start):
    ts = h.shape[0]
    width = win_ref.shape[1] // 2
    n_blocks = width // RNN_BLOCK_W
    n_groups = ts // SUBLANES

    hn = _rms(h, g_ref[...]).astype(BF16)
    y = _gelu_tanh(_dot(hn, win_ref[:, :width]))
    yield
    xr = _dot(hn, win_ref[:, width:]).reshape(n_groups, SUBLANES, width)

    sub = lax.broadcasted_iota(jnp.int32, (1, SUBLANES, RNN_BLOCK_W), 1)
    not_first_w = lax.broadcasted_iota(jnp.int32, (1, SUBLANES, width), 1) >= 1

    def vec(i):
        return vec_ref[i:i + 1, :].reshape(1, 1, width)

    xc = vec(0) * xr
    for w in range(1, CONV_WIDTH):
        tail = jnp.where(is_start, 0.0, tail_ref[w - 1])
        ext = jnp.concatenate([tail[None], xc], axis=0)
        tail_ref[w - 1] = xc[n_groups - 1]
        rot = pltpu.roll(ext, 1, axis=1)
        xc = vec(w) * xr + jnp.where(not_first_w, rot[1:], rot[:-1])
    xc = (xc + vec(CONV_WIDTH)).reshape(ts, width)
    yield

    bias_x = -LOG2_E * vec_ref[5:6, :]
    bias_a = -LOG2_E * vec_ref[6:7, :]
    log2_a_unit = (RG_C * LOG2_E) * _log_sigmoid(vec_ref[7:8, :])
    start = jnp.logical_and(sub == 0, is_start)
    for n in range(n_blocks):
        cols = slice(n * RNN_BLOCK_W, (n + 1) * RNN_BLOCK_W)
        xcn = xc[:, cols]
        g = _dot(xcn.astype(BF16), wxa_ref[n])
        gate_x = _sigmoid_exp2(g[:, :RNN_BLOCK_W] + bias_x[:, cols])
        gate_a = _sigmoid_exp2(g[:, RNN_BLOCK_W:] + bias_a[:, cols])
        a = jnp.exp2(gate_a * log2_a_unit[:, cols]).reshape(n_groups, SUBLANES, RNN_BLOCK_W)
        a = jnp.concatenate([jnp.where(start, 0.0, a[:1]), a[1:]], axis=0)
        mult = jnp.exp2(0.5 * jnp.log2(1.0 - a * a))
        b = mult * (gate_x * xcn).reshape(n_groups, SUBLANES, RNN_BLOCK_W)
        k = 1
        while k < SUBLANES:
            keep = sub >= k
            b = b + jnp.where(keep, a, 0.0) * pltpu.roll(b, k, axis=1)
            a = a * jnp.where(keep, pltpu.roll(a, k, axis=1), 1.0)
            k *= 2
        a_buf[:, :, cols] = a
        b_buf[:, :, cols] = b
        yield

    carry = jnp.where(is_start, 0.0, hstate[...])
    for gidx in range(n_groups):
        rows = slice(gidx * SUBLANES, (gidx + 1) * SUBLANES)
        hg = a_buf[gidx] * carry + b_buf[gidx]
        o_ref[rows, :] = (hg * y[rows, :]).astype(o_ref.dtype)
        carry = hg[SUBLANES - 1:SUBLANES, :]
    hstate[...] = carry


FF_CHUNK = 1024


def _post_stages(a, h, p, g_ref, wout_ref, wup_ref, wdn_ref, wproj_ref, wgate_ref, o_ref, *, ff_chunk=FF_CHUNK):
    m = _dot(a, wout_ref[...])
    h1 = h + _rms(m, g_ref[0:1, :])
    u = _rms(h1, g_ref[1:2, :]).astype(BF16)
    yield
    f = jnp.zeros_like(h1)
    for c in range(wup_ref.shape[1] // ff_chunk):
        cols = slice(c * ff_chunk, (c + 1) * ff_chunk)
        t = jnp.square(jnp.maximum(_dot(u, wup_ref[:, cols]), 0.0)).astype(BF16)
        f = f + _dot(t, wdn_ref[cols, :])
        yield
    h2 = h1 + _rms(f, g_ref[2:3, :])
    e = _rms(_dot(p.astype(BF16), wproj_ref[...]), g_ref[3:4, :])
    gate = jax.nn.sigmoid(_dot(h2.astype(BF16), wgate_ref[...]))
    o_ref[...] = h2 + e * gate


def _post_kernel(a_ref, h_ref, p_ref, g_ref, wout_ref, wup_ref, wdn_ref, wproj_ref, wgate_ref, o_ref):
    _run(_post_stages(a_ref[...], h_ref[...], p_ref[...], g_ref, wout_ref, wup_ref, wdn_ref,
                      wproj_ref, wgate_ref, o_ref))


def _post(a, h, p, gains, w_out, w_up, w_down, w_proj, w_gate, *, layer, tm):
    T, D = h.shape
    row = lambda i: (i, 0)
    return pl.pallas_call(
        _post_kernel,
        grid=(T // tm,),
        in_specs=[
            pl.BlockSpec((tm, a.shape[1]), row),
            pl.BlockSpec((tm, D), row),
            pl.BlockSpec((None, tm, p.shape[2]), lambda i: (layer, i, 0)),
            _resident(gains.shape),
            _resident(w_out.shape),
            _resident(w_up.shape),
            _resident(w_down.shape),
            _resident(w_proj.shape),
            _resident(w_gate.shape),
        ],
        out_specs=pl.BlockSpec((tm, D), row),
        out_shape=jax.ShapeDtypeStruct((T, D), F32),
        compiler_params=_params(("parallel",)),
        name="post",
    )(a, h, p, gains, w_out, w_up, w_down, w_proj, w_gate)


REC_POST_ORDER = "rpr" + "p" + "rp" + "rrp" + "rp" + "rrp" + "rp" + "rrp" + "rp" + "r"

def _rec_post_kernel(h_cur_ref, h_prev_ref, p_ref, gr_ref, win_ref, vec_ref, wxa_ref, g_ref, wout_ref, wup_ref,
                     wdn_ref, wproj_ref, wgate_ref, o_ref, hy_ref, hy_prev_ref, tail_ref, a_buf, b_buf, hstate,
                     *, tiles_per_seq):
    i = pl.program_id(0)

    @pl.when(i == 0)
    def _():
        hy_ref[...] = jnp.zeros(hy_ref.shape, hy_ref.dtype)

    hy_prev_ref[...] = hy_ref[...]
    tile = jnp.minimum(i, pl.num_programs(0) - 2)
    rec = _rec_stages(h_cur_ref[...], gr_ref, win_ref, vec_ref, wxa_ref, hy_ref, tail_ref, a_buf, b_buf, hstate,
                      is_start=tile % tiles_per_seq == 0)
    post = _post_stages(hy_prev_ref[...], h_prev_ref[...], p_ref[...], g_ref, wout_ref, wup_ref, wdn_ref,
                        wproj_ref, wgate_ref, o_ref, ff_chunk=FF_CHUNK // 2)
    for turn in REC_POST_ORDER:
        next(rec if turn == "r" else post, None)
    _run(rec)
    _run(post)


def _rec_post(h, p, gain, w_in, vecs, wxa, gains, w_out, w_up, w_down, w_proj, w_gate, *, layer, S, tm):
    T, D = h.shape
    width = w_in.shape[1] // 2
    n = T // tm
    cur = lambda i: (jnp.minimum(i, n - 1), 0)
    prev = lambda i: (jnp.maximum(i - 1, 0), 0)
    return pl.pallas_call(
        functools.partial(_rec_post_kernel, tiles_per_seq=S // tm),
        grid=(n + 1,),
        in_specs=[
            pl.BlockSpec((tm, D), cur),
            pl.BlockSpec((tm, D), prev),
            pl.BlockSpec((None, tm, p.shape[2]), lambda i: (layer, jnp.maximum(i - 1, 0), 0)),
            _resident((1, D)),
            _resident(w_in.shape),
            _resident(vecs.shape),
            _resident(wxa.shape),
            _resident(gains.shape),
            _resident(w_out.shape),
            _resident(w_up.shape),
            _resident(w_down.shape),
            _resident(w_proj.shape),
            _resident(w_gate.shape),
        ],
        out_specs=pl.BlockSpec((tm, D), prev),
        out_shape=jax.ShapeDtypeStruct((T, D), F32),
        scratch_shapes=[
            pltpu.VMEM((tm, width), BF16),
            pltpu.VMEM((tm, width), BF16),
            pltpu.VMEM((CONV_WIDTH - 1, SUBLANES, width), F32),
            pltpu.VMEM((tm // SUBLANES, SUBLANES, width), F32),
            pltpu.VMEM((tm // SUBLANES, SUBLANES, width), F32),
            pltpu.VMEM((1, width), F32),
        ],
        compiler_params=_params(("arbitrary",)),
        name="rec_post",
    )(h, h, p, gain, w_in, vecs, wxa, gains, w_out, w_up, w_down, w_proj, w_gate)


def _pad_rows(x, rows):
    return jnp.concatenate([x, jnp.zeros((rows - x.shape[0],) + x.shape[1:], x.dtype)], axis=0)


def _post_operands(layer, ln_mix_post, ln_mlp_pre, ln_mlp_post, ple_norm, w_out,
                   mlp_w_up, mlp_w_down, ple_w_proj, ple_w_gate):
    gains = _pad_rows(jnp.stack([ln_mix_post[layer], ln_mlp_pre[layer], ln_mlp_post[layer],
                                 ple_norm[layer]]), SUBLANES)
    return (gains, w_out.astype(BF16), mlp_w_up[layer].astype(BF16), mlp_w_down[layer].astype(BF16),
            ple_w_proj[layer].astype(BF16), ple_w_gate[layer].astype(BF16))


def kernel(x, p, ln_mix_pre, ln_mix_post, ln_mlp_pre, ln_mlp_post, mlp_w_up, mlp_w_down, ple_w_proj, ple_norm, ple_w_gate, attn_w_in, attn_b_forget, attn_w_out, diff_lambda_q1, diff_lambda_k1, diff_lambda_q2, diff_lambda_k2, diff_subln, rec_w_in, rec_conv_w, rec_conv_b, rec_wx, rec_bx, rec_wa, rec_ba, rec_a_param, rec_w_out):
    B, S, D = x.shape
    T = B * S
    depth = p.shape[0]
    tm = 512
    n_diff_heads = D // (4 * HEAD_DIM)
    n_fox_heads = D // (2 * HEAD_DIM)
    n_units = n_diff_heads + n_fox_heads // 2
    qkv_width = 3 * (n_diff_heads + n_fox_heads // 2) * UNIT
    assert S % (2 * tm) == 0 and attn_w_in.shape[2] == qkv_width + n_fox_heads == qkv_width + SUBLANES

    h = x.reshape(T, D)
    p = p.reshape(depth, T, -1)
    for layer in range(depth):
        j = layer // 2
        gain = ln_mix_pre[layer].reshape(1, D)
        post_operands = functools.partial(_post_operands, layer, ln_mix_post, ln_mlp_pre, ln_mlp_post, ple_norm,
                                          mlp_w_up=mlp_w_up, mlp_w_down=mlp_w_down, ple_w_proj=ple_w_proj,
                                          ple_w_gate=ple_w_gate)
        if layer % 2 == 0:
            w_in = attn_w_in[j]
            col = jnp.arange(qkv_width)
            is_q = (col < n_diff_heads * UNIT) | ((col >= 3 * n_diff_heads * UNIT)
                                                  & (col < 3 * n_diff_heads * UNIT + n_fox_heads * HEAD_DIM))
            w_main = (w_in[:, :qkv_width] * jnp.where(is_q, LOG2_E * HEAD_DIM ** -0.5, 1.0)).astype(BF16)
            w_fz_t = _pad_rows(w_in[:, qkv_width:].T, 2 * SUBLANES).astype(BF16)
            z, lf_t = _attn_in(h, gain, w_main, w_fz_t, attn_b_forget[j].reshape(n_fox_heads, 1), tm=tm)
            key_bias = _key_bias(lf_t, B=B, S=S, n_diff_heads=n_diff_heads, n_units=n_units)
            lam_params = _pad_rows(jnp.pad(
                jnp.stack([diff_lambda_q1[j], diff_lambda_k1[j], diff_lambda_q2[j], diff_lambda_k2[j]]),
                ((0, 0), (0, LANES - HEAD_DIM))), SUBLANES)
            lam_init = 0.8 - 0.6 * math.exp(-0.3 * layer)
            a = _attention(z, key_bias, lam_params, diff_subln[j].reshape(1, UNIT), B=B, S=S, tq=2 * tm, ups=2,
                           n_diff_units=n_diff_heads, n_units=n_units, lam_init=lam_init)
            h = _post(a, h, p, *post_operands(w_out=attn_w_out[j]), layer=layer, tm=tm)
        else:
            vecs = jnp.concatenate([rec_conv_w[j], rec_conv_b[j][None], rec_bx[j][None],
                                    rec_ba[j][None], rec_a_param[j][None]], axis=0)
            wxa = (-LOG2_E * jnp.concatenate([rec_wx[j], rec_wa[j]], axis=-1)).astype(BF16)
            h = _rec_post(h, p, gain, rec_w_in[j].astype(BF16), vecs, wxa, *post_operands(w_out=rec_w_out[j]),
                          layer=layer, S=S, tm=tm)
    return h.reshape(B, S, D)
```
